```python
import math
import functools
import jax
import jax.numpy as jnp
from jax import lax
import numpy as np

D_MODEL = 1024
BATCH = 1
SEQ = 16384
DEPTH = 1
DEC_BATCH = 32
DEC_SEQ = 4
PAST_LEN = 16384
PAGE_SIZE = 128

SSM_EXPAND = 2
D_INNER = SSM_EXPAND * D_MODEL
SSM_HEAD_DIM = 64
SSM_HEADS = D_INNER // SSM_HEAD_DIM
SSM_GROUPS = 4
SSM_STATE = 128
CONV_WIDTH = 4
CONV_DIM = D_INNER + 2 * SSM_GROUPS * SSM_STATE
SSD_CHUNK = 128
ATTN_HEADS = 16
ATTN_HEAD_DIM = 64
ATTN_DIM = ATTN_HEADS * ATTN_HEAD_DIM
MOBA_BLOCK = 256
MOBA_TOPK = 3
MOBA_Q_BLOCK = 64
N_MEM = 256
XATTN_HEADS = 4
XATTN_HEAD_DIM = D_MODEL // XATTN_HEADS
D_FF = 4 * D_MODEL
NORM_EPS = 1e-6
Z_OFF = 0
XBC_OFF = Z_OFF + D_INNER
DT_OFF = XBC_OFF + CONV_DIM
Q_OFF = DT_OFF + SSM_HEADS
K_OFF = Q_OFF + ATTN_DIM
V_OFF = K_OFF + ATTN_DIM
G_OFF = V_OFF + ATTN_DIM
IN_DIM = G_OFF + 2 * D_MODEL

kernel_name = "hybrid_ssd_moba_decoder_step"


def _rmsnorm(x, g):
    xf = x.astype(jnp.float32)
    y = xf * lax.rsqrt(jnp.mean(xf * xf, axis=-1, keepdims=True) + NORM_EPS)
    return (y * g.astype(jnp.float32)).astype(x.dtype)


def _alibi_slopes(n_heads):
    return 2.0 ** (-8.0 * jnp.arange(1, n_heads + 1, dtype=jnp.float32) / n_heads)


def _causal_conv(u, conv_state, w, b):
    L = u.shape[1]
    up = jnp.concatenate([conv_state.astype(u.dtype), u], axis=1)
    acc = b
    for j in range(CONV_WIDTH):
        acc = acc + up[:, j:j + L] * w[j]
    return jax.nn.silu(acc), up[:, L:]


def _segsum(a):
    T = a.shape[-1]
    ab = jnp.broadcast_to(a[..., :, None], a.shape + (T,))
    strict = jnp.tril(jnp.ones((T, T), dtype=bool), k=-1)
    cs = jnp.cumsum(jnp.where(strict, ab, 0.0), axis=-2)
    return jnp.where(jnp.tril(jnp.ones((T, T), dtype=bool)), cs, -jnp.inf)


def _ssd_scan(x, dt, a, bm, cm, h0):
    f32 = jnp.float32
    Bsz, L, H, P = x.shape
    G, N = bm.shape[2], bm.shape[3]
    R = H // G
    T = SSD_CHUNK if L >= SSD_CHUNK else L
    Lp = -(-L // T) * T
    xf, dtf, bf, cf = x.astype(f32), dt.astype(f32), bm.astype(f32), cm.astype(f32)
    if Lp != L:
        pw = ((0, 0), (0, Lp - L))
        xf = jnp.pad(xf, pw + ((0, 0), (0, 0)))
        dtf = jnp.pad(dtf, pw + ((0, 0),))
        bf = jnp.pad(bf, pw + ((0, 0), (0, 0)))
        cf = jnp.pad(cf, pw + ((0, 0), (0, 0)))
    nc = Lp // T
    X = (xf * dtf[..., None]).reshape(Bsz, nc, T, G, R, P)
    Adt = (dtf * a).reshape(Bsz, nc, T, G, R).transpose(0, 3, 4, 1, 2)
    Bc = bf.reshape(Bsz, nc, T, G, N)
    Cc = cf.reshape(Bsz, nc, T, G, N)
    A_cs = jnp.cumsum(Adt, axis=-1)
    Lmat = jnp.exp(_segsum(Adt))
    CB = jnp.einsum("bclgn,bcsgn->bgcls", Cc, Bc)
    y_diag = jnp.einsum("bgcls,bgrcls,bcsgrp->bclgrp", CB, Lmat, X)
    decay_states = jnp.exp(A_cs[..., -1:] - A_cs)
    states = jnp.einsum("bclgn,bgrcl,bclgrp->bcgrpn", Bc, decay_states, X)
    h0r = h0.astype(f32).reshape(Bsz, 1, G, R, P, N)
    states = jnp.concatenate([h0r, states], axis=1)
    chunk_decay = jnp.exp(_segsum(jnp.pad(A_cs[..., -1], ((0, 0), (0, 0), (0, 0), (1, 0)))))
    new_states = jnp.einsum("bgrzc,bcgrpn->bzgrpn", chunk_decay, states)
    states_in, h_final = new_states[:, :-1], new_states[:, -1]
    y_off = jnp.einsum("bclgn,bcgrpn,bgrcl->bclgrp", Cc, states_in, jnp.exp(A_cs))
    y = (y_diag + y_off).reshape(Bsz, Lp, H, P)[:, :L]
    return y, h_final.reshape(Bsz, H, P, N)


def _ssd_branch(u_z, u_xbc, u_dt, conv_state, h0, conv_w, conv_b, dt_bias, a_log, d_skip, g_ssm_norm):
    f32 = jnp.float32
    Bsz, L, _ = u_z.shape
    xbc, new_conv = _causal_conv(u_xbc, conv_state, conv_w, conv_b)
    gn = SSM_GROUPS * SSM_STATE
    xs = xbc[..., :D_INNER].reshape(Bsz, L, SSM_HEADS, SSM_HEAD_DIM)
    bm = xbc[..., D_INNER:D_INNER + gn].reshape(Bsz, L, SSM_GROUPS, SSM_STATE)
    cm = xbc[..., D_INNER + gn:].reshape(Bsz, L, SSM_GROUPS, SSM_STATE)
    dt = jax.nn.softplus(u_dt.astype(f32) + dt_bias.astype(f32))
    a = -jnp.exp(a_log.astype(f32))
    y, h = _ssd_scan(xs, dt, a, bm, cm, h0)
    y = y + xs.astype(f32) * d_skip.astype(f32)[:, None]
    y = y.reshape(Bsz, L, D_INNER) * jax.nn.silu(u_z.astype(f32))
    yg = y.reshape(Bsz, L, SSM_GROUPS, D_INNER // SSM_GROUPS)
    yg = yg * lax.rsqrt(jnp.mean(yg * yg, axis=-1, keepdims=True) + NORM_EPS)
    y = yg.reshape(Bsz, L, D_INNER) * g_ssm_norm.astype(f32)
    return y.astype(u_z.dtype), new_conv, h


def _combine(parts, out_shape):
    p = jax.nn.softmax(jnp.concatenate([s for s, _, _ in parts], axis=-1), axis=-1)
    o = jnp.zeros(out_shape, jnp.float32)
    off = 0
    for s, vv, per_query in parts:
        n = s.shape[-1]
        pp = p[..., off:off + n].astype(vv.dtype)
        eq = "bqhs,bqhsd->bqhd" if per_query else "bqhs,bshd->bqhd"
        o = o + jnp.einsum(eq, pp, vv, preferred_element_type=jnp.float32)
        off += n
    return o


def _moba_prompt(q, k, v):
    f32 = jnp.float32
    Bsz, S, H, dh = q.shape
    nb = -(-S // MOBA_BLOCK)
    Sp = nb * MOBA_BLOCK
    if Sp != S:
        pw = ((0, 0), (0, Sp - S), (0, 0), (0, 0))
        q, k, v = jnp.pad(q, pw), jnp.pad(k, pw), jnp.pad(v, pw)
    kb = k.reshape(Bsz, nb, MOBA_BLOCK, H, dh)
    vb = v.reshape(Bsz, nb, MOBA_BLOCK, H, dh)
    kmean = jnp.mean(kb.astype(f32), axis=2)
    ksel = min(MOBA_TOPK, nb - 1)
    slopes = _alibi_slopes(H)
    scale = dh ** -0.5
    b_ix = jnp.arange(Bsz)[:, None, None, None]
    h_ix = jnp.arange(H)[None, None, :, None]
    offs = jnp.arange(MOBA_BLOCK)

    def one_block(q0):
        qs = lax.dynamic_slice_in_dim(q, q0, MOBA_Q_BLOCK, axis=1)
        t = q0 + jnp.arange(MOBA_Q_BLOCK)
        blk = q0 // MOBA_BLOCK
        own0 = blk * MOBA_BLOCK
        parts = []
        if ksel > 0:
            gate = jnp.einsum("bqhd,bjhd->bqhj", qs.astype(f32), kmean)
            gate = jnp.where(jnp.arange(nb) < blk, gate, -jnp.inf)
            _, idx = lax.top_k(gate, ksel)
            k_g = kb[b_ix, idx, :, h_ix].reshape(Bsz, MOBA_Q_BLOCK, H, ksel * MOBA_BLOCK, dh)
            v_g = vb[b_ix, idx, :, h_ix].reshape(Bsz, MOBA_Q_BLOCK, H, ksel * MOBA_BLOCK, dh)
            s = jnp.einsum("bqhd,bqhsd->bqhs", qs, k_g, preferred_element_type=f32) * scale
            kpos = (idx[..., None] * MOBA_BLOCK + offs).reshape(Bsz, MOBA_Q_BLOCK, H, ksel * MOBA_BLOCK)
            s = s - slopes[:, None] * (t[:, None, None] - kpos).astype(f32)
            valid = jnp.repeat(idx < blk, MOBA_BLOCK, axis=-1)
            parts.append((jnp.where(valid, s, -jnp.inf), v_g, True))
        k_o = lax.dynamic_slice_in_dim(k, own0, MOBA_BLOCK, axis=1)
        v_o = lax.dynamic_slice_in_dim(v, own0, MOBA_BLOCK, axis=1)
        s_o = jnp.einsum("bqhd,bshd->bqhs", qs, k_o, preferred_element_type=f32) * scale
        d_o = t[:, None] - (own0 + offs)[None, :]
        s_o = s_o - slopes[:, None] * d_o[:, None, :].astype(f32)
        parts.append((jnp.where((d_o >= 0)[:, None, :], s_o, -jnp.inf), v_o, False))
        return _combine(parts, qs.shape)

    out = lax.map(one_block, jnp.arange(0, Sp, MOBA_Q_BLOCK))
    out = jnp.moveaxis(out, 0, 1).reshape(Bsz, Sp, H, dh)[:, :S]
    return out


def _moba_sample(q, k_new, v_new, cache_k, cache_v, page_ksum, page_table, lay):
    f32 = jnp.float32
    Bd, T, H, dh = q.shape
    n_pages = page_table.shape[1]
    past = n_pages * PAGE_SIZE
    ppb = MOBA_BLOCK // PAGE_SIZE
    blk = past // MOBA_BLOCK
    own0 = blk * MOBA_BLOCK
    n_own = n_pages - blk * ppb
    ksel = min(MOBA_TOPK, blk)
    slopes = _alibi_slopes(H)
    scale = dh ** -0.5
    t = past + jnp.arange(T)
    parts = []
    if ksel > 0:
        kmean = page_ksum[page_table[:, :blk * ppb]]
        kmean = kmean.reshape(Bd, blk, ppb, H, dh).sum(axis=2) / MOBA_BLOCK
        gate = jnp.einsum("bqhd,bjhd->bqhj", q.astype(f32), kmean)
        _, idx = lax.top_k(gate, ksel)
        logical = idx[..., None] * ppb + jnp.arange(ppb)
        phys = page_table[jnp.arange(Bd)[:, None, None, None, None], logical]
        h_ix = jnp.arange(H)[None, None, :, None, None]
        k_g = cache_k[lay, phys, :, h_ix].reshape(Bd, T, H, ksel * MOBA_BLOCK, dh)
        v_g = cache_v[lay, phys, :, h_ix].reshape(Bd, T, H, ksel * MOBA_BLOCK, dh)
        s = jnp.einsum("bqhd,bqhsd->bqhs", q, k_g, preferred_element_type=f32) * scale
        kpos = (idx[..., None] * MOBA_BLOCK + jnp.arange(MOBA_BLOCK)).reshape(Bd, T, H, ksel * MOBA_BLOCK)
        s = s - slopes[:, None] * (t[:, None, None] - kpos).astype(f32)
        parts.append((s, v_g, True))
    if n_own > 0:
        phys_o = page_table[:, blk * ppb:]
        k_o = cache_k[lay, phys_o].reshape(Bd, n_own * PAGE_SIZE, H, dh)
        v_o = cache_v[lay, phys_o].reshape(Bd, n_own * PAGE_SIZE, H, dh)
        s_o = jnp.einsum("bqhd,bshd->bqhs", q, k_o, preferred_element_type=f32) * scale
        d_o = t[:, None] - (own0 + jnp.arange(n_own * PAGE_SIZE))[None, :]
        parts.append((s_o - slopes[:, None] * d_o[:, None, :].astype(f32), v_o, False))
    s_n = jnp.einsum("bqhd,bshd->bqhs", q, k_new, preferred_element_type=f32) * scale
    d_n = t[:, None] - t[None, :]
    s_n = s_n - slopes[:, None] * d_n[:, None, :].astype(f32)
    parts.append((jnp.where((d_n >= 0)[:, None, :], s_n, -jnp.inf), v_new, False))
    return _combine(parts, q.shape)


def _mem_kv(mem, g_mem, w_xk, w_xv):
    Bsz, M, _ = mem.shape
    m = _rmsnorm(mem, g_mem)
    mk = (m @ w_xk).reshape(Bsz, M, XATTN_HEADS, XATTN_HEAD_DIM)
    mv = (m @ w_xv).reshape(Bsz, M, XATTN_HEADS, XATTN_HEAD_DIM)
    return mk, mv


def _cross_attn(xn, mk, mv, w_xq, w_xo):
    Bsz, L, _ = xn.shape
    q = (xn @ w_xq).reshape(Bsz, L, XATTN_HEADS, XATTN_HEAD_DIM)
    s = jnp.einsum("blhd,bmhd->blhm", q, mk, preferred_element_type=jnp.float32) * XATTN_HEAD_DIM ** -0.5
    p = jax.nn.softmax(s, axis=-1).astype(mv.dtype)
    o = jnp.einsum("blhm,bmhd->blhd", p, mv).reshape(Bsz, L, D_MODEL)
    return o @ w_xo


def _layer(x, attn_fn, conv_state, ssm_h0, mem_k, mem_v,
           g_norm_mix, w_in, conv_w, conv_b, dt_bias, a_log, d_skip, g_ssm_norm,
           w_ssm_proj, w_attn_proj, w_out, g_norm_x, w_xq, w_xo, g_norm_ffn, w_up, w_down):
    Bsz, L, _ = x.shape
    u = _rmsnorm(x, g_norm_mix) @ w_in
    q = u[..., Q_OFF:K_OFF].reshape(Bsz, L, ATTN_HEADS, ATTN_HEAD_DIM)
    k = u[..., K_OFF:V_OFF].reshape(Bsz, L, ATTN_HEADS, ATTN_HEAD_DIM)
    v = u[..., V_OFF:G_OFF].reshape(Bsz, L, ATTN_HEADS, ATTN_HEAD_DIM)
    y_ssm, new_conv, new_h = _ssd_branch(u[..., Z_OFF:XBC_OFF], u[..., XBC_OFF:DT_OFF], u[..., DT_OFF:Q_OFF],
                                         conv_state, ssm_h0, conv_w, conv_b, dt_bias, a_log, d_skip, g_ssm_norm)
    o_attn = attn_fn(q, k, v).astype(x.dtype).reshape(Bsz, L, ATTN_DIM)
    gates = jax.nn.sigmoid(u[..., G_OFF:].astype(jnp.float32)).astype(x.dtype)
    merged = gates[..., :D_MODEL] * (y_ssm @ w_ssm_proj) + gates[..., D_MODEL:] * (o_attn @ w_attn_proj)
    x = x + merged @ w_out
    x = x + _cross_attn(_rmsnorm(x, g_norm_x), mem_k, mem_v, w_xq, w_xo)
    hid = jax.nn.relu(_rmsnorm(x, g_norm_ffn) @ w_up)
    x = x + (hid * hid) @ w_down
    return x, k, v, new_conv, new_h


def setup_inputs(seed: int = 0) -> dict:
    key = jax.random.key(seed)
    f32 = jnp.float32
    counter = [0]

    def nk():
        counter[0] += 1
        return jax.random.fold_in(key, counter[0])

    def nrm(shape, scale=1.0):
        return jax.random.normal(nk(), shape, f32) * scale

    def gain(shape):
        return 1.0 + 0.01 * jax.random.normal(nk(), shape, f32)

    n_pages = PAST_LEN // PAGE_SIZE
    n_used = DEC_BATCH * n_pages
    n_pool = n_used + max(1, n_used // 4)
    page_table = jax.random.permutation(nk(), n_pool)[:n_used].reshape(DEC_BATCH, n_pages).astype(jnp.int32)
    dt0 = jnp.exp(jax.random.uniform(nk(), (DEPTH, SSM_HEADS), f32, math.log(1e-3), math.log(1e-1)))
    dt_bias = dt0 + jnp.log(-jnp.expm1(-dt0))
    a_log = jnp.log(jax.random.uniform(nk(), (DEPTH, SSM_HEADS), f32, 1.0, 16.0))
    return {
        "x_prompt": nrm((BATCH, SEQ, D_MODEL)),
        "x_sample": nrm((DEC_BATCH, DEC_SEQ, D_MODEL)),
        "state_conv": nrm((DEPTH, DEC_BATCH, CONV_WIDTH - 1, CONV_DIM)),
        "state_ssm": nrm((DEPTH, DEC_BATCH, SSM_HEADS, SSM_HEAD_DIM, SSM_STATE), 0.5),
        "cache_k": nrm((DEPTH, n_pool, PAGE_SIZE, ATTN_HEADS, ATTN_HEAD_DIM)),
        "cache_v": nrm((DEPTH, n_pool, PAGE_SIZE, ATTN_HEADS, ATTN_HEAD_DIM)),
        "cache_mem_k": nrm((DEPTH, DEC_BATCH, N_MEM, XATTN_HEADS, XATTN_HEAD_DIM)),
        "cache_mem_v": nrm((DEPTH, DEC_BATCH, N_MEM, XATTN_HEADS, XATTN_HEAD_DIM)),
        "page_table": page_table,
        "mem_prompt": nrm((BATCH, N_MEM, D_MODEL)),
        "g_norm_mix": gain((DEPTH, D_MODEL)),
        "w_in": nrm((DEPTH, D_MODEL, IN_DIM), D_MODEL ** -0.5),
        "conv_w": nrm((DEPTH, CONV_WIDTH, CONV_DIM), CONV_WIDTH ** -0.5),
        "conv_b": nrm((DEPTH, CONV_DIM), 0.01),
        "dt_bias": dt_bias,
        "a_log": a_log,
        "d_skip": 1.0 + 0.1 * jax.random.normal(nk(), (DEPTH, SSM_HEADS), f32),
        "g_ssm_norm": gain((DEPTH, D_INNER)),
        "w_ssm_proj": nrm((DEPTH, D_INNER, D_MODEL), D_INNER ** -0.5),
        "w_attn_proj": nrm((DEPTH, ATTN_DIM, D_MODEL), ATTN_DIM ** -0.5),
        "w_out": nrm((DEPTH, D_MODEL, D_MODEL), D_MODEL ** -0.5),
        "g_norm_x": gain((DEPTH, D_MODEL)),
        "g_mem": gain((DEPTH, D_MODEL)),
        "w_xq": nrm((DEPTH, D_MODEL, D_MODEL), D_MODEL ** -0.5),
        "w_xk": nrm((DEPTH, D_MODEL, D_MODEL), D_MODEL ** -0.5),
        "w_xv": nrm((DEPTH, D_MODEL, D_MODEL), D_MODEL ** -0.5),
        "w_xo": nrm((DEPTH, D_MODEL, D_MODEL), D_MODEL ** -0.5),
        "g_norm_ffn": gain((DEPTH, D_MODEL)),
        "w_up": nrm((DEPTH, D_MODEL, D_FF), D_MODEL ** -0.5),
        "w_down": nrm((DEPTH, D_FF, D_MODEL), D_FF ** -0.5),
        "g_final": gain((D_MODEL,)),
    }


def reference(x_prompt, x_sample, state_conv, state_ssm, cache_k, cache_v, cache_mem_k, cache_mem_v,
              page_table, mem_prompt, g_norm_mix, w_in, conv_w, conv_b, dt_bias, a_log, d_skip,
              g_ssm_norm, w_ssm_proj, w_attn_proj, w_out, g_norm_x, g_mem, w_xq, w_xk, w_xv, w_xo,
              g_norm_ffn, w_up, w_down, g_final):
    bp = x_prompt.shape[0]
    page_ksum = jnp.sum(cache_k, axis=2, dtype=jnp.float32)
    xp, xs = x_prompt, x_sample
    kp_l, vp_l, cp_l, hp_l, mkp_l, mvp_l = [], [], [], [], [], []
    ks_l, vs_l, cs_l, hs_l = [], [], [], []
    for l in range(DEPTH):
        lw = (g_norm_mix[l], w_in[l], conv_w[l], conv_b[l], dt_bias[l], a_log[l], d_skip[l], g_ssm_norm[l],
              w_ssm_proj[l], w_attn_proj[l], w_out[l], g_norm_x[l], w_xq[l], w_xo[l], g_norm_ffn[l],
              w_up[l], w_down[l])
        mk_p, mv_p = _mem_kv(mem_prompt, g_mem[l], w_xk[l], w_xv[l])
        conv0 = jnp.zeros((bp, CONV_WIDTH - 1, CONV_DIM), x_prompt.dtype)
        h0 = jnp.zeros((bp, SSM_HEADS, SSM_HEAD_DIM, SSM_STATE), jnp.float32)
        xp, kp, vp, cp, hp = _layer(xp, _moba_prompt, conv0, h0, mk_p, mv_p, *lw)
        attn_s = functools.partial(_moba_sample, cache_k=cache_k, cache_v=cache_v, page_ksum=page_ksum[l],
                                   page_table=page_table, lay=jnp.asarray(l, dtype=jnp.int32))
        xs, ks, vs, cs, hs = _layer(xs, attn_s, state_conv[l], state_ssm[l], cache_mem_k[l], cache_mem_v[l], *lw)
        kp_l.append(kp); vp_l.append(vp); cp_l.append(cp); hp_l.append(hp)
        mkp_l.append(mk_p); mvp_l.append(mv_p)
        ks_l.append(ks); vs_l.append(vs); cs_l.append(cs); hs_l.append(hs)
    y_prompt = _rmsnorm(xp, g_final)
    y_sample = _rmsnorm(xs, g_final)
    return (y_prompt, y_sample,
            jnp.stack(kp_l), jnp.stack(vp_l), jnp.stack(cp_l), jnp.stack(hp_l), jnp.stack(mkp_l), jnp.stack(mvp_l),
            jnp.stack(ks_l), jnp.stack(vs_l), jnp.stack(cs_l), jnp.stack(hs_l))
```

```python
import functools
import math

import jax
import jax.numpy as jnp
from jax import lax
from jax.experimental import pallas as pl
from jax.experimental.pallas import tpu as pltpu

F32 = jnp.float32
BF16 = jnp.bfloat16

NORM_EPS = 1e-6
SSM_HEAD_DIM = 64
SSM_GROUPS = 4
SSM_STATE = 128
CONV_WIDTH = 4
SSD_CHUNK = 128
ATTN_HEAD_DIM = 64
MOBA_BLOCK = 256
MOBA_TOPK = 3
PAGE_SIZE = 128
XATTN_HEADS = 4

LANES = 128
SUBLANES = 8
VMEM_LIMIT = 56 * 1024 * 1024
NEG_INF = float("-inf")


def _params(sem):
    return pltpu.CompilerParams(dimension_semantics=sem, vmem_limit_bytes=VMEM_LIMIT)


def _sigmoid(x):
    return 1.0 / (1.0 + jnp.exp(-x))


def _silu(x):
    return x * _sigmoid(x)


def _row_tile(m, cap):
    t = min(m, cap)
    while m % t:
        t //= 2
    return t


def _rmsnorm_kernel(x_ref, g_ref, o_ref):
    x = x_ref[...]
    ms = jnp.mean(x * x, axis=-1, keepdims=True)
    o_ref[...] = (x * lax.rsqrt(ms + NORM_EPS) * g_ref[...]).astype(o_ref.dtype)


def _rmsnorm(x, g, out_dtype):
    m, d = x.shape
    tm = _row_tile(m, 1024)
    return pl.pallas_call(
        _rmsnorm_kernel,
        out_shape=jax.ShapeDtypeStruct((m, d), out_dtype),
        grid=(m // tm,),
        in_specs=[pl.BlockSpec((tm, d), lambda i: (i, 0)), pl.BlockSpec((1, d), lambda i: (0, 0))],
        out_specs=pl.BlockSpec((tm, d), lambda i: (i, 0)),
        compiler_params=_params(("parallel",)),
        name="rmsnorm",
    )(x, g.reshape(1, d))


def _matmul_kernel(a_ref, w_ref, o_ref, *, act):
    acc = jnp.dot(a_ref[...], w_ref[...], preferred_element_type=F32)
    if act == "relu2":
        acc = jnp.maximum(acc, 0.0)
        acc = acc * acc
    o_ref[...] = acc.astype(o_ref.dtype)


def _matmul(a, w, out_dtype, act=None, tm_cap=1024, tn_cap=1024):
    m, k = a.shape
    n = w.shape[1]
    tm = _row_tile(m, tm_cap)
    tn = _row_tile(n, tn_cap)
    return pl.pallas_call(
        functools.partial(_matmul_kernel, act=act),
        out_shape=jax.ShapeDtypeStruct((m, n), out_dtype),
        grid=(m // tm, n // tn),
        in_specs=[pl.BlockSpec((tm, k), lambda i, j: (i, 0)), pl.BlockSpec((k, tn), lambda i, j: (0, j))],
        out_specs=pl.BlockSpec((tm, tn), lambda i, j: (i, j)),
        compiler_params=_params(("parallel", "parallel")),
        name="matmul",
    )(a, w)


def _matmul_res_kernel(a_ref, w_ref, r_ref, o_ref):
    o_ref[...] = r_ref[...] + jnp.dot(a_ref[...], w_ref[...], preferred_element_type=F32)


def _matmul_res(a, w, res):
    m, k = a.shape
    n = w.shape[1]
    tm = _row_tile(m, 512)
    return pl.pallas_call(
        _matmul_res_kernel,
        out_shape=jax.ShapeDtypeStruct((m, n), F32),
        grid=(m // tm,),
        in_specs=[pl.BlockSpec((tm, k), lambda i: (i, 0)), pl.BlockSpec((k, n), lambda i: (0, 0)),
                  pl.BlockSpec((tm, n), lambda i: (i, 0))],
        out_specs=pl.BlockSpec((tm, n), lambda i: (i, 0)),
        compiler_params=_params(("parallel",)),
        name="matmul_res",
    )(a, w, res)


def _matmul_res_norm_kernel(a_ref, w_ref, r_ref, g_ref, o_ref):
    x = r_ref[...] + jnp.dot(a_ref[...], w_ref[...], preferred_element_type=F32)
    ms = jnp.mean(x * x, axis=-1, keepdims=True)
    o_ref[...] = x * lax.rsqrt(ms + NORM_EPS) * g_ref[...]


def _matmul_res_norm(a, w, res, g):
    m, k = a.shape
    n = w.shape[1]
    tm = _row_tile(m, 512)
    return pl.pallas_call(
        _matmul_res_norm_kernel,
        out_shape=jax.ShapeDtypeStruct((m, n), F32),
        grid=(m // tm,),
        in_specs=[pl.BlockSpec((tm, k), lambda i: (i, 0)), pl.BlockSpec((k, n), lambda i: (0, 0)),
                  pl.BlockSpec((tm, n), lambda i: (i, 0)), pl.BlockSpec((1, n), lambda i: (0, 0))],
        out_specs=pl.BlockSpec((tm, n), lambda i: (i, 0)),
        compiler_params=_params(("parallel",)),
        name="matmul_res_norm",
    )(a, w, res, g.reshape(1, n))


def _merge_kernel(ys_ref, oa_ref, gl_ref, x_ref, wsp_ref, wap_ref, wo_ref, o_ref):
    d = o_ref.shape[-1]
    gates = _sigmoid(gl_ref[...])
    a = jnp.dot(ys_ref[...], wsp_ref[...], preferred_element_type=F32)
    b = jnp.dot(oa_ref[...], wap_ref[...], preferred_element_type=F32)
    merged = gates[:, :d] * a + gates[:, d:] * b
    o_ref[...] = x_ref[...] + jnp.dot(merged.astype(BF16), wo_ref[...], preferred_element_type=F32)


def _merge(y_ssm, o_attn, gate_logits, x, w_sp, w_ap, w_o):
    m, d = x.shape
    tm = _row_tile(m, 256)
    row = lambda i: (i, 0)
    full = lambda i: (0, 0)
    return pl.pallas_call(
        _merge_kernel,
        out_shape=jax.ShapeDtypeStruct((m, d), F32),
        grid=(m // tm,),
        in_specs=[pl.BlockSpec((tm, y_ssm.shape[1]), row), pl.BlockSpec((tm, o_attn.shape[1]), row),
                  pl.BlockSpec((tm, 2 * d), row), pl.BlockSpec((tm, d), row),
                  pl.BlockSpec(w_sp.shape, full), pl.BlockSpec(w_ap.shape, full), pl.BlockSpec(w_o.shape, full)],
        out_specs=pl.BlockSpec((tm, d), row),
        compiler_params=_params(("parallel",)),
        name="merge",
    )(y_ssm, o_attn, gate_logits, x, w_sp, w_ap, w_o)


def _cumsum_rows(x):
    t = x.shape[0]
    row = lax.broadcasted_iota(jnp.int32, x.shape, 0)
    k = 1
    while k < t:
        x = x + jnp.where(row >= k, pltpu.roll(x, k, axis=0), 0.0)
        k *= 2
    return x


def _ssd_kernel(z_ref, xbc_ref, dtp_ref, cs_ref, h0_ref, cw_ref, cb_ref, dtb_ref, alog_ref, dsk_ref, gn_ref,
                y_ref, nc_ref, h_ref, cbuf, acs_s, acst_s, dt_s, ybuf, *, t_in, t_c):
    c = pl.program_id(1)
    n_heads = h_ref.shape[1]
    hd = SSM_HEAD_DIM
    d_inner = n_heads * hd
    gn = SSM_GROUPS * SSM_STATE
    hpg = n_heads // SSM_GROUPS
    pad0 = SUBLANES
    tail = CONV_WIDTH - 1

    @pl.when(c == 0)
    def _():
        h_ref[...] = h0_ref[...]
        cbuf[...] = jnp.zeros(cbuf.shape, F32)
        cbuf[pad0 - tail:pad0, :] = cs_ref[0]

    cbuf[pad0:pad0 + t_in, :] = xbc_ref[0]
    acc = cb_ref[...]
    for j in range(CONV_WIDTH):
        acc = acc + cbuf[pad0 - tail + j:pad0 - tail + j + t_c, :] * cw_ref[j:j + 1, :]
    xbc = _silu(acc)
    new_tail = cbuf[pad0 + t_in - tail:pad0 + t_in, :]
    nc_ref[0] = new_tail
    cbuf[pad0 - tail:pad0, :] = new_tail

    rows = lax.broadcasted_iota(jnp.int32, (t_c, LANES), 0)
    dtp = dtp_ref[0]
    if t_c != t_in:
        dtp = jnp.concatenate([dtp, jnp.zeros((t_c - t_in, LANES), F32)], axis=0)
    pre = dtp + dtb_ref[...]
    dt = jnp.maximum(pre, 0.0) + jnp.log(1.0 + jnp.exp(-jnp.abs(pre)))
    dt = jnp.where(rows < t_in, dt, 0.0)
    a = -jnp.exp(alog_ref[...])
    acs = _cumsum_rows(dt * a)
    dt_s[...] = dt
    acs_s[...] = acs
    acst_s[...] = acs.T

    li = lax.broadcasted_iota(jnp.int32, (t_c, t_c), 0)
    si = lax.broadcasted_iota(jnp.int32, (t_c, t_c), 1)
    tril = li >= si
    a_last = acs_s[t_c - 1:t_c, :]

    for g in range(SSM_GROUPS):
        b_g = xbc[:, d_inner + g * SSM_STATE:d_inner + (g + 1) * SSM_STATE].astype(BF16)
        c_g = xbc[:, d_inner + gn + g * SSM_STATE:d_inner + gn + (g + 1) * SSM_STATE].astype(BF16)
        cb = lax.dot_general(c_g, b_g, (((1,), (1,)), ((), ())), preferred_element_type=F32)
        for r in range(hpg):
            h = g * hpg + r
            a_col = acs_s[:, h:h + 1]
            a_row = acst_s[h:h + 1, :]
            dt_col = dt_s[:, h:h + 1]
            decay = jnp.where(tril, jnp.exp(jnp.minimum(a_col - a_row, 0.0)), 0.0)
            x_h = xbc[:, h * hd:(h + 1) * hd]
            xd = x_h * dt_col
            y_diag = jnp.dot((cb * decay).astype(BF16), xd.astype(BF16), preferred_element_type=F32)
            s_h = h_ref[0, h]
            y_off = lax.dot_general(c_g, s_h.astype(BF16), (((1,), (1,)), ((), ())),
                                    preferred_element_type=F32)
            ybuf[:, h * hd:(h + 1) * hd] = y_diag + y_off * jnp.exp(a_col) + x_h * dsk_ref[:, h * hd:(h + 1) * hd]
            to_end = jnp.exp(a_last[:, h:h + 1] - a_col)
            upd = lax.dot_general((xd * to_end).astype(BF16), b_g, (((0,), (0,)), ((), ())),
                                  preferred_element_type=F32)
            h_ref[0, h] = s_h * jnp.exp(a_last[:, h:h + 1]) + upd

    z = z_ref[0]
    if t_c != t_in:
        z = jnp.concatenate([z, jnp.zeros((t_c - t_in, d_inner), F32)], axis=0)
    y = ybuf[...] * _silu(z)
    gw = d_inner // SSM_GROUPS
    parts = []
    for g in range(SSM_GROUPS):
        yg = y[:, g * gw:(g + 1) * gw]
        parts.append(yg * lax.rsqrt(jnp.mean(yg * yg, axis=-1, keepdims=True) + NORM_EPS))
    y = jnp.concatenate(parts, axis=-1) * gn_ref[...]
    y_ref[0] = y[:t_in].astype(y_ref.dtype)


def _ssd_branch(z, xbc, dtp, conv_state, h0, conv_w, conv_b, dt_bias, a_log, d_skip, g_ssm_norm):
    bsz, seq, d_inner = z.shape
    conv_dim = xbc.shape[-1]
    n_heads = h0.shape[1]
    t_in = min(SSD_CHUNK, seq)
    assert seq % t_in == 0
    t_c = -(-t_in // SUBLANES) * SUBLANES
    nchunk = seq // t_in
    pad_h = LANES - n_heads
    dtb = jnp.pad(dt_bias, (0, pad_h)).reshape(1, LANES)
    alog = jnp.pad(a_log, (0, pad_h)).reshape(1, LANES)
    dsk = jnp.repeat(d_skip, SSM_HEAD_DIM).reshape(1, d_inner)
    seq_blk = lambda w: pl.BlockSpec((1, t_in, w), lambda b, c: (b, c, 0))
    per_b3 = lambda s: pl.BlockSpec((1,) + s, lambda b, c: (b,) + (0,) * len(s))
    const2 = lambda s: pl.BlockSpec(s, lambda b, c: (0, 0))
    kern = functools.partial(_ssd_kernel, t_in=t_in, t_c=t_c)
    return pl.pallas_call(
        kern,
        out_shape=(jax.ShapeDtypeStruct((bsz, seq, d_inner), BF16),
                   jax.ShapeDtypeStruct((bsz, CONV_WIDTH - 1, conv_dim), F32),
                   jax.ShapeDtypeStruct(h0.shape, F32)),
        grid=(bsz, nchunk),
        in_specs=[seq_blk(d_inner), seq_blk(conv_dim), seq_blk(LANES),
                  per_b3((CONV_WIDTH - 1, conv_dim)), per_b3(h0.shape[1:]),
                  const2((CONV_WIDTH, conv_dim)), const2((1, conv_dim)), const2((1, LANES)), const2((1, LANES)),
                  const2((1, d_inner)), const2((1, d_inner))],
        out_specs=(seq_blk(d_inner), per_b3((CONV_WIDTH - 1, conv_dim)), per_b3(h0.shape[1:])),
        scratch_shapes=[pltpu.VMEM((SUBLANES + t_c, conv_dim), F32),
                        pltpu.VMEM((t_c, LANES), F32), pltpu.VMEM((LANES, t_c), F32), pltpu.VMEM((t_c, LANES), F32),
                        pltpu.VMEM((t_c, d_inner), F32)],
        compiler_params=_params(("parallel", "arbitrary")),
        name="ssd_branch",
    )(z, xbc, dtp, conv_state, h0, conv_w, conv_b.reshape(1, conv_dim), dtb, alog, dsk,
      g_ssm_norm.reshape(1, d_inner))


def _kmean_kernel(k_ref, km_ref, kb_ref):
    i = pl.program_id(0)
    k = k_ref[...]
    km_ref[pl.ds(i, 1), :] = jnp.sum(k, axis=0, keepdims=True) * (1.0 / MOBA_BLOCK)
    kb_ref[...] = k.astype(BF16)


def _kmean(k):
    seq, hd = k.shape
    nb = seq // MOBA_BLOCK
    return pl.pallas_call(
        _kmean_kernel,
        out_shape=(jax.ShapeDtypeStruct((nb, hd), F32), jax.ShapeDtypeStruct((seq, hd), BF16)),
        grid=(nb,),
        in_specs=[pl.BlockSpec((MOBA_BLOCK, hd), lambda i: (i, 0))],
        out_specs=(pl.BlockSpec((nb, hd), lambda i: (0, 0)), pl.BlockSpec((MOBA_BLOCK, hd), lambda i: (i, 0))),
        compiler_params=_params(("arbitrary",)),
        name="moba_kmean",
    )(k)


def _moba_prompt_kernel(qt_ref, k_ref, vt_ref, km_ref, offb_ref, ownb_ref, slope_ref, o_ref,
                        qm_s, sel_s, m_s, l_s, acc_s):
    i = pl.program_id(1)
    nb = km_ref.shape[0]
    blk = MOBA_BLOCK
    hd = ATTN_HEAD_DIM
    heads = qt_ref.shape[0] // hd
    scale = hd ** -0.5
    lane_rows = lax.broadcasted_iota(jnp.int32, qt_ref.shape, 0)
    blk_ids = lax.broadcasted_iota(jnp.int32, (nb, blk), 0)

    for hh in range(heads):
        in_head = (lane_rows >= hh * hd) & (lane_rows < (hh + 1) * hd)
        qm = jnp.where(in_head, qt_ref[...], jnp.zeros_like(qt_ref[...]))
        qm_s[hh] = (qm.astype(F32) * scale).astype(BF16)
        gate = jnp.dot(km_ref[...], qm.astype(F32), preferred_element_type=F32,
                       precision=lax.Precision.HIGHEST)
        g = jnp.where(blk_ids < i, gate, NEG_INF)
        sel = jnp.zeros((nb, blk), jnp.bool_)
        for _ in range(MOBA_TOPK):
            top = jnp.max(g, axis=0, keepdims=True)
            first = jnp.min(jnp.where(g == top, blk_ids, nb), axis=0, keepdims=True)
            pick = blk_ids == first
            sel = sel | pick
            g = jnp.where(pick, NEG_INF, g)
        sel = sel & (blk_ids < i)
        sel_s[hh] = jnp.where(sel, 0.0, NEG_INF)

        s_t = jnp.dot(k_ref[pl.ds(i * blk, blk), :], qm_s[hh], preferred_element_type=F32) + ownb_ref[hh]
        m0 = jnp.max(s_t, axis=0, keepdims=True)
        p = jnp.exp(s_t - m0)
        m_s[hh] = m0
        l_s[hh] = jnp.sum(p, axis=0, keepdims=True)
        acc_s[hh] = jnp.dot(vt_ref[hh, i], p.astype(BF16), preferred_element_type=F32)

    def body(j, carry):
        dist = ((i - j) * blk).astype(F32)
        for hh in range(heads):
            z = jnp.dot(k_ref[pl.ds(j * blk, blk), :], qm_s[hh], preferred_element_type=F32) + offb_ref[hh]
            c_j = sel_s[hh, pl.ds(j, 1), :] - slope_ref[hh] * dist
            m_old = m_s[hh]
            m_new = jnp.maximum(m_old, jnp.max(z, axis=0, keepdims=True) + c_j)
            alpha = jnp.exp(m_old - m_new)
            p = jnp.exp(z + (c_j - m_new))
            l_s[hh] = alpha * l_s[hh] + jnp.sum(p, axis=0, keepdims=True)
            acc_s[hh] = alpha * acc_s[hh] + jnp.dot(vt_ref[hh, j], p.astype(BF16), preferred_element_type=F32)
            m_s[hh] = m_new
        return carry

    lax.fori_loop(0, i, body, 0)

    for hh in range(heads):
        o_ref[hh * hd:(hh + 1) * hd, :] = (acc_s[hh] / l_s[hh]).astype(o_ref.dtype)


def _alibi_slopes(n_heads):
    return 2.0 ** (-8.0 * jnp.arange(1, n_heads + 1, dtype=F32) / n_heads)


def _moba_prompt(q_t, k_bf, v_t4, kmean, n_heads):
    hd = ATTN_HEAD_DIM
    blk = MOBA_BLOCK
    seq = k_bf.shape[0]
    nb = seq // blk
    hps = LANES // hd
    slopes = _alibi_slopes(n_heads)
    offs = jnp.arange(blk, dtype=F32)
    offb = jnp.broadcast_to((slopes[:, None] * offs[None, :])[:, :, None], (n_heads, blk, blk))
    causal = jnp.where(offs[:, None] <= offs[None, :], 0.0, NEG_INF)
    ownb = offb + causal[None]
    slope_rows = jnp.broadcast_to(slopes[:, None, None], (n_heads, 1, blk))
    return pl.pallas_call(
        _moba_prompt_kernel,
        out_shape=jax.ShapeDtypeStruct((n_heads * hd, seq), BF16),
        grid=(n_heads // hps, nb),
        in_specs=[pl.BlockSpec((hps * hd, blk), lambda g, i: (g, i)),
                  pl.BlockSpec((seq, hps * hd), lambda g, i: (0, g)),
                  pl.BlockSpec((hps, nb, hd, blk), lambda g, i: (g, 0, 0, 0)),
                  pl.BlockSpec((nb, hps * hd), lambda g, i: (0, g)),
                  pl.BlockSpec((hps, blk, blk), lambda g, i: (g, 0, 0)),
                  pl.BlockSpec((hps, blk, blk), lambda g, i: (g, 0, 0)),
                  pl.BlockSpec((hps, 1, blk), lambda g, i: (g, 0, 0))],
        out_specs=pl.BlockSpec((hps * hd, blk), lambda g, i: (g, i)),
        scratch_shapes=[pltpu.VMEM((hps, hps * hd, blk), BF16), pltpu.VMEM((hps, nb, blk), F32),
                        pltpu.VMEM((hps, 1, blk), F32), pltpu.VMEM((hps, 1, blk), F32),
                        pltpu.VMEM((hps, hd, blk), F32)],
        compiler_params=_params(("parallel", "arbitrary")),
        name="moba_prompt",
    )(q_t, k_bf, v_t4, kmean, offb, ownb, slope_rows)


def _xattn_kernel(q_ref, mk_ref, mv_ref, o_ref):
    dh = q_ref.shape[-1] // XATTN_HEADS
    scale = dh ** -0.5
    q = q_ref[0]
    for h in range(XATTN_HEADS):
        sl = slice(h * dh, (h + 1) * dh)
        s = lax.dot_general(q[:, sl], mk_ref[0, :, sl].astype(BF16), (((1,), (1,)), ((), ())),
                            preferred_element_type=F32) * scale
        p = jnp.exp(s - jnp.max(s, axis=-1, keepdims=True))
        p = p / jnp.sum(p, axis=-1, keepdims=True)
        o_ref[0, :, sl] = jnp.dot(p.astype(BF16), mv_ref[0, :, sl].astype(BF16),
                                  preferred_element_type=F32).astype(o_ref.dtype)


def _xattn(q, mk, mv):
    bsz, seq, d = q.shape
    n_mem = mk.shape[1]
    tm = _row_tile(seq, 512)
    return pl.pallas_call(
        _xattn_kernel,
        out_shape=jax.ShapeDtypeStruct((bsz, seq, d), BF16),
        grid=(bsz, seq // tm),
        in_specs=[pl.BlockSpec((1, tm, d), lambda b, i: (b, i, 0)),
                  pl.BlockSpec((1, n_mem, d), lambda b, i: (b, 0, 0)),
                  pl.BlockSpec((1, n_mem, d), lambda b, i: (b, 0, 0))],
        out_specs=pl.BlockSpec((1, tm, d), lambda b, i: (b, i, 0)),
        compiler_params=_params(("parallel", "parallel")),
        name="xattn",
    )(q, mk, mv)


def _page_sum_kernel(ck_ref, o_ref):
    o_ref[...] = jnp.sum(ck_ref[0], axis=1)


def _page_sums(cache_k, lay):
    _, n_pool, page, n_heads, hd = cache_k.shape
    pp = _row_tile(n_pool, 8)
    return pl.pallas_call(
        _page_sum_kernel,
        out_shape=jax.ShapeDtypeStruct((n_pool, n_heads, hd), F32),
        grid=(n_pool // pp,),
        in_specs=[pl.BlockSpec((1, pp, page, n_heads, hd), lambda i: (lay, i, 0, 0, 0))],
        out_specs=pl.BlockSpec((pp, n_heads, hd), lambda i: (i, 0, 0)),
        compiler_params=_params(("parallel",)),
        name="page_sums",
    )(cache_k)


def _sample_gate_kernel(pt_ref, qe_ref, ksum_hbm, idx_ref, buf, sem, *, pages_per_block):
    b = pl.program_id(0)
    n_pages = buf.shape[0]
    nblk = n_pages // pages_per_block

    def page_copy(pg):
        return pltpu.make_async_copy(ksum_hbm.at[pl.ds(pt_ref[b * n_pages + pg], 1)], buf.at[pl.ds(pg, 1)], sem)

    def start(pg, carry):
        page_copy(pg).start()
        return carry

    def wait(pg, carry):
        page_copy(pg).wait()
        return carry

    lax.fori_loop(0, n_pages, start, 0)
    lax.fori_loop(0, n_pages, wait, 0)

    blk_id = lax.broadcasted_iota(jnp.int32, (nblk, n_pages), 0)
    page_id = lax.broadcasted_iota(jnp.int32, (nblk, n_pages), 1)
    pool = jnp.where(page_id // pages_per_block == blk_id, 1.0 / MOBA_BLOCK, 0.0)
    kmean = jnp.dot(pool, buf[...], preferred_element_type=F32, precision=lax.Precision.HIGHEST)
    gate = lax.dot_general(qe_ref[0], kmean, (((1,), (1,)), ((), ())), preferred_element_type=F32,
                           precision=lax.Precision.HIGHEST)
    lane = lax.broadcasted_iota(jnp.int32, gate.shape, 1)
    out_lane = lax.broadcasted_iota(jnp.int32, idx_ref.shape[1:], 1)
    out = jnp.zeros(idx_ref.shape[1:], jnp.int32)
    for s in range(MOBA_TOPK):
        top = jnp.max(gate, axis=-1, keepdims=True)
        first = jnp.min(jnp.where(gate == top, lane, nblk), axis=-1, keepdims=True)
        out = jnp.where(out_lane == s, first, out)
        gate = jnp.where(lane == first, NEG_INF, gate)
    idx_ref[0] = out


def _sample_gate(page_table, q_exp, page_ksum):
    bsz, rows, hd_all = q_exp.shape
    n_pages = page_table.shape[1]
    ppb = MOBA_BLOCK // PAGE_SIZE
    kern = functools.partial(_sample_gate_kernel, pages_per_block=ppb)
    return pl.pallas_call(
        kern,
        out_shape=jax.ShapeDtypeStruct((bsz, rows, LANES), jnp.int32),
        grid_spec=pltpu.PrefetchScalarGridSpec(
            num_scalar_prefetch=1,
            grid=(bsz,),
            in_specs=[pl.BlockSpec((1, rows, hd_all), lambda b, pt: (b, 0, 0)),
                      pl.BlockSpec(memory_space=pltpu.HBM)],
            out_specs=pl.BlockSpec((1, rows, LANES), lambda b, pt: (b, 0, 0)),
            scratch_shapes=[pltpu.VMEM((n_pages, hd_all), F32), pltpu.SemaphoreType.DMA]),
        compiler_params=_params(("arbitrary",)),
        name="sample_gate",
    )(page_table.reshape(-1), q_exp, page_ksum.reshape(page_ksum.shape[0], hd_all))


def _sample_attn_kernel(pt_ref, idx_ref, q_ref, kn_ref, vn_ref, slope_ref, ck_hbm, cv_hbm, o_ref,
                        kbuf, vbuf, sem, *, lay, n_tok, n_pages, past):
    step = pl.program_id(0)
    n_steps = pl.num_programs(0)
    n_heads = ck_hbm.shape[3]
    hd = ck_hbm.shape[4]
    ppb = MOBA_BLOCK // PAGE_SIZE
    n_seg = n_tok * MOBA_TOPK
    n_keys = n_seg * MOBA_BLOCK
    rows_pad = SUBLANES

    def block_index(st, seg):
        b, h = st // n_heads, st % n_heads
        t, s = seg // MOBA_TOPK, seg % MOBA_TOPK
        return idx_ref[((b * n_tok + t) * n_heads + h) * MOBA_TOPK + s]

    def page_copies(st, slot):
        b, h = st // n_heads, st % n_heads
        out = []
        for seg in range(n_seg):
            blk = block_index(st, seg)
            for pg in range(ppb):
                phys = pt_ref[b * n_pages + blk * ppb + pg]
                dst = pl.ds((seg * ppb + pg) * PAGE_SIZE, PAGE_SIZE)
                out.append(pltpu.make_async_copy(ck_hbm.at[lay, phys, :, h, :], kbuf.at[slot, dst, :], sem.at[0, slot]))
                out.append(pltpu.make_async_copy(cv_hbm.at[lay, phys, :, h, :], vbuf.at[slot, dst, :], sem.at[1, slot]))
        return out

    slot = step % 2

    @pl.when(step == 0)
    def _():
        for cp in page_copies(step, slot):
            cp.start()

    @pl.when(step + 1 < n_steps)
    def _():
        for cp in page_copies(step + 1, 1 - slot):
            cp.start()

    for cp in page_copies(step, slot):
        cp.wait()

    slope = slope_ref[0]
    scale = hd ** -0.5
    zpad = jnp.zeros((rows_pad - n_tok, hd), F32)
    q = (jnp.concatenate([q_ref[0, 0], zpad], axis=0) * scale).astype(BF16)
    kn = jnp.concatenate([kn_ref[0, 0], zpad], axis=0).astype(BF16)
    vn = jnp.concatenate([vn_ref[0, 0], zpad], axis=0).astype(BF16)

    s_g = lax.dot_general(q, kbuf[slot].astype(BF16), (((1,), (1,)), ((), ())), preferred_element_type=F32)
    col = lax.broadcasted_iota(jnp.int32, (1, n_keys), 1)
    seg_of_col = col // MOBA_BLOCK
    blk_of_col = jnp.zeros((1, n_keys), jnp.int32)
    for seg in range(n_seg):
        blk_of_col = jnp.where(seg_of_col == seg, block_index(step, seg), blk_of_col)
    kpos = blk_of_col * MOBA_BLOCK + col % MOBA_BLOCK
    row = lax.broadcasted_iota(jnp.int32, (rows_pad, n_keys), 0)
    dist = (past + row - kpos).astype(F32)
    s_g = jnp.where(row == seg_of_col // MOBA_TOPK, s_g - slope[:, :1] * dist, NEG_INF)

    s_n = lax.dot_general(q, kn, (((1,), (1,)), ((), ())), preferred_element_type=F32)
    r_n = lax.broadcasted_iota(jnp.int32, s_n.shape, 0)
    c_n = lax.broadcasted_iota(jnp.int32, s_n.shape, 1)
    s_n = jnp.where((c_n <= r_n) & (c_n < n_tok), s_n - slope[:, :1] * (r_n - c_n).astype(F32), NEG_INF)

    m = jnp.maximum(jnp.max(s_g, axis=-1, keepdims=True), jnp.max(s_n, axis=-1, keepdims=True))
    p_g = jnp.exp(s_g - m)
    p_n = jnp.exp(s_n - m)
    den = jnp.sum(p_g, axis=-1, keepdims=True) + jnp.sum(p_n, axis=-1, keepdims=True)
    o = jnp.dot(p_g.astype(BF16), vbuf[slot].astype(BF16), preferred_element_type=F32)
    o = o + jnp.dot(p_n.astype(BF16), vn, preferred_element_type=F32)
    o_ref[0, 0] = (o / den)[:n_tok].astype(o_ref.dtype)


def _sample_attn(page_table, idx_flat, q_h, k_h, v_h, cache_k, cache_v, lay):
    bsz, n_heads, n_tok, hd = q_h.shape
    n_pages = page_table.shape[1]
    past = n_pages * PAGE_SIZE
    n_keys = n_tok * MOBA_TOPK * MOBA_BLOCK
    slopes = jnp.broadcast_to(_alibi_slopes(n_heads)[:, None, None], (n_heads, 1, LANES))
    tok = pl.BlockSpec((1, 1, n_tok, hd), lambda s, pt, ix: (s // n_heads, s % n_heads, 0, 0))
    kern = functools.partial(_sample_attn_kernel, lay=lay, n_tok=n_tok, n_pages=n_pages, past=past)
    return pl.pallas_call(
        kern,
        out_shape=jax.ShapeDtypeStruct((bsz, n_heads, n_tok, hd), BF16),
        grid_spec=pltpu.PrefetchScalarGridSpec(
            num_scalar_prefetch=2,
            grid=(bsz * n_heads,),
            in_specs=[tok, tok, tok,
                      pl.BlockSpec((1, 1, LANES), lambda s, pt, ix: (s % n_heads, 0, 0)),
                      pl.BlockSpec(memory_space=pltpu.HBM), pl.BlockSpec(memory_space=pltpu.HBM)],
            out_specs=tok,
            scratch_shapes=[pltpu.VMEM((2, n_keys, hd), F32), pltpu.VMEM((2, n_keys, hd), F32),
                            pltpu.SemaphoreType.DMA((2, 2))]),
        compiler_params=_params(("arbitrary",)),
        name="sample_attn",
    )(page_table.reshape(-1), idx_flat, q_h, k_h, v_h, slopes, cache_k, cache_v)


def _moba_sample(q, k_new, v_new, cache_k, cache_v, page_ksum, page_table, lay):
    bsz, n_tok, hd_all = q.shape
    n_heads = cache_k.shape[3]
    hd = cache_k.shape[4]
    n_pages = page_table.shape[1]
    ppb = MOBA_BLOCK // PAGE_SIZE
    assert n_pages % ppb == 0, "past pages inside the new tokens' own block are not supported"
    assert n_pages // ppb >= MOBA_TOPK and n_tok <= SUBLANES
    head_of_lane = jnp.arange(hd_all) // hd
    head_mask = (head_of_lane[None, :] == jnp.arange(n_heads)[:, None]).astype(F32)
    q_exp = (q[:, :, None, :] * head_mask[None, None]).reshape(bsz, n_tok * n_heads, hd_all)
    idx = _sample_gate(page_table, q_exp, page_ksum)[:, :, :MOBA_TOPK]
    heads_first = lambda a: a.reshape(bsz, n_tok, n_heads, hd).transpose(0, 2, 1, 3)
    o = _sample_attn(page_table, idx.reshape(-1), heads_first(q), heads_first(k_new), heads_first(v_new),
                     cache_k, cache_v, lay)
    return o.transpose(0, 2, 1, 3).reshape(bsz, n_tok, hd_all)


def _prompt_attn(q, k, v, n_heads):
    seq = q.shape[0]
    assert seq % MOBA_BLOCK == 0
    nb = seq // MOBA_BLOCK
    kmean, k_bf = _kmean(k)
    v_t4 = v.astype(BF16).reshape(nb, MOBA_BLOCK, n_heads, ATTN_HEAD_DIM).transpose(2, 0, 3, 1)
    return _moba_prompt(q.T, k_bf, v_t4, kmean, n_heads).T


def _layer(x, attn_fn, conv_state, h0, mem_k, mem_v, w, g_final):
    bsz, seq, d = x.shape
    rows = bsz * seq
    x2 = x.reshape(rows, d)
    xn = _rmsnorm(x2, w["g_norm_mix"], BF16)
    z = _matmul(xn, w["w_z"], F32)
    xbc = _matmul(xn, w["w_xbc"], F32)
    dtp = _matmul(xn, w["w_dt"], F32)
    q = _matmul(xn, w["w_q"], w["q_dtype"])
    k = _matmul(xn, w["w_k"], F32)
    v = _matmul(xn, w["w_v"], F32)
    gate_logits = _matmul(xn, w["w_g"], F32)
    y_ssm, new_conv, new_h = _ssd_branch(
        z.reshape(bsz, seq, -1), xbc.reshape(bsz, seq, -1), dtp.reshape(bsz, seq, -1), conv_state, h0,
        w["conv_w"], w["conv_b"], w["dt_bias"], w["a_log"], w["d_skip"], w["g_ssm_norm"])
    o_attn = attn_fn(q.reshape(bsz, seq, -1), k.reshape(bsz, seq, -1), v.reshape(bsz, seq, -1))
    x2 = _merge(y_ssm.reshape(rows, -1), o_attn.reshape(rows, -1), gate_logits, x2,
                w["w_ssm_proj"], w["w_attn_proj"], w["w_out"])
    xq = _matmul(_rmsnorm(x2, w["g_norm_x"], BF16), w["w_xq"], BF16)
    o_x = _xattn(xq.reshape(bsz, seq, d), mem_k, mem_v)
    x2 = _matmul_res(o_x.reshape(rows, d), w["w_xo"], x2)
    hid = _matmul(_rmsnorm(x2, w["g_norm_ffn"], BF16), w["w_up"], BF16, act="relu2")
    if g_final is None:
        x2 = _matmul_res(hid, w["w_down"], x2)
    else:
        x2 = _matmul_res_norm(hid, w["w_down"], x2, g_final)
    return x2.reshape(bsz, seq, d), k, v, new_conv, new_h


def kernel(x_prompt, x_sample, state_conv, state_ssm, cache_k, cache_v, cache_mem_k, cache_mem_v, page_table, mem_prompt, g_norm_mix, w_in, conv_w, conv_b, dt_bias, a_log, d_skip, g_ssm_norm, w_ssm_proj, w_attn_proj, w_out, g_norm_x, g_mem, w_xq, w_xk, w_xv, w_xo, g_norm_ffn, w_up, w_down, g_final):
    depth = w_in.shape[0]
    bp, seq_p, d = x_prompt.shape
    bd, seq_d, _ = x_sample.shape
    n_heads, hd = cache_k.shape[3], cache_k.shape[4]
    attn_dim = n_heads * hd
    ssm_heads = dt_bias.shape[1]
    d_inner = ssm_heads * SSM_HEAD_DIM
    conv_dim = conv_w.shape[2]
    n_mem = mem_prompt.shape[1]
    xh, xd = cache_mem_k.shape[3], cache_mem_k.shape[4]
    z_off = 0
    xbc_off = z_off + d_inner
    dt_off = xbc_off + conv_dim
    q_off = dt_off + ssm_heads
    k_off = q_off + attn_dim
    v_off = k_off + attn_dim
    g_off = v_off + attn_dim

    xp, xs = x_prompt, x_sample
    outs = {n: [] for n in ("kp", "vp", "cp", "hp", "mkp", "mvp", "ks", "vs", "cs", "hs")}
    for l in range(depth):
        wl = w_in[l].astype(BF16)
        w = {
            "g_norm_mix": g_norm_mix[l], "g_norm_x": g_norm_x[l], "g_norm_ffn": g_norm_ffn[l],
            "w_z": wl[:, z_off:xbc_off], "w_xbc": wl[:, xbc_off:dt_off],
            "w_dt": jnp.pad(wl[:, dt_off:q_off], ((0, 0), (0, LANES - ssm_heads))),
            "w_q": wl[:, q_off:k_off], "w_k": wl[:, k_off:v_off], "w_v": wl[:, v_off:g_off], "w_g": wl[:, g_off:],
            "conv_w": conv_w[l], "conv_b": conv_b[l], "dt_bias": dt_bias[l], "a_log": a_log[l],
            "d_skip": d_skip[l], "g_ssm_norm": g_ssm_norm[l],
            "w_ssm_proj": w_ssm_proj[l].astype(BF16), "w_attn_proj": w_attn_proj[l].astype(BF16),
            "w_out": w_out[l].astype(BF16), "w_xq": w_xq[l].astype(BF16), "w_xo": w_xo[l].astype(BF16),
            "w_up": w_up[l].astype(BF16), "w_down": w_down[l].astype(BF16),
        }
        last = g_final if l == depth - 1 else None

        mem_n = _rmsnorm(mem_prompt.reshape(bp * n_mem, d), g_mem[l], BF16)
        mk_p = _matmul(mem_n, w_xk[l].astype(BF16), F32).reshape(bp, n_mem, d)
        mv_p = _matmul(mem_n, w_xv[l].astype(BF16), F32).reshape(bp, n_mem, d)
        conv0 = jnp.zeros((bp, CONV_WIDTH - 1, conv_dim), F32)
        h0 = jnp.zeros((bp, ssm_heads, SSM_HEAD_DIM, SSM_STATE), F32)
        attn_p = lambda q, k, v: jnp.stack([_prompt_attn(q[b], k[b], v[b], n_heads) for b in range(bp)])
        xp, kp, vp, cp, hp = _layer(xp, attn_p, conv0, h0, mk_p, mv_p, dict(w, q_dtype=BF16), last)

        page_ksum = _page_sums(cache_k, l)
        attn_s = lambda q, k, v: _moba_sample(q, k, v, cache_k, cache_v, page_ksum, page_table, l)
        xs, ks, vs, cs, hs = _layer(xs, attn_s, state_conv[l], state_ssm[l],
                                    cache_mem_k[l].reshape(bd, n_mem, d), cache_mem_v[l].reshape(bd, n_mem, d),
                                    dict(w, q_dtype=F32), last)

        outs["kp"].append(kp.reshape(bp, seq_p, n_heads, hd))
        outs["vp"].append(vp.reshape(bp, seq_p, n_heads, hd))
        outs["cp"].append(cp)
        outs["hp"].append(hp)
        outs["mkp"].append(mk_p.reshape(bp, n_mem, xh, xd))
        outs["mvp"].append(mv_p.reshape(bp, n_mem, xh, xd))
        outs["ks"].append(ks.reshape(bd, seq_d, n_heads, hd))
        outs["vs"].append(vs.reshape(bd, seq_d, n_heads, hd))
        outs["cs"].append(cs)
        outs["hs"].append(hs)

    if depth == 0:
        raise ValueError("at least one layer is required")
    st = lambda n: jnp.stack(outs[n])
    return (xp, xs, st("kp"), st("vp"), st("cp"), st("hp"), st("mkp"), st("mvp"),
            st("ks"), st("vs"), st("cs"), st("hs"))
```

```python
import functools
import math

import jax
import jax.numpy as jnp
from jax import lax
from jax.experimental import pallas as pl
from jax.experimental.pallas import tpu as pltpu

F32 = jnp.float32
BF16 = jnp.bfloat16

NORM_EPS = 1e-6
SSM_HEAD_DIM = 64
SSM_GROUPS = 4
SSM_STATE = 128
CONV_WIDTH = 4
SSD_CHUNK = 128
ATTN_HEAD_DIM = 64
MOBA_BLOCK = 256
MOBA_TOPK = 3
PAGE_SIZE = 128
XATTN_HEADS = 4

LANES = 128
SUBLANES = 8
VMEM_LIMIT = 56 * 1024 * 1024
NEG_INF = float("-inf")
LOG2E = math.log2(math.e)


def _params(sem):
    return pltpu.CompilerParams(dimension_semantics=sem, vmem_limit_bytes=VMEM_LIMIT)


def _sigmoid(x):
    return 1.0 / (1.0 + jnp.exp(-x))


def _silu(x):
    return x * _sigmoid(x)


def _row_tile(m, cap):
    t = min(m, cap)
    while m % t:
        t //= 2
    return t


def _rmsnorm_kernel(x_ref, g_ref, o_ref):
    x = x_ref[...]
    ms = jnp.mean(x * x, axis=-1, keepdims=True)
    o_ref[...] = (x * lax.rsqrt(ms + NORM_EPS) * g_ref[...]).astype(o_ref.dtype)


def _rmsnorm(x, g, out_dtype):
    m, d = x.shape
    tm = _row_tile(m, 1024)
    return pl.pallas_call(
        _rmsnorm_kernel,
        out_shape=jax.ShapeDtypeStruct((m, d), out_dtype),
        grid=(m // tm,),
        in_specs=[pl.BlockSpec((tm, d), lambda i: (i, 0)), pl.BlockSpec((1, d), lambda i: (0, 0))],
        out_specs=pl.BlockSpec((tm, d), lambda i: (i, 0)),
        compiler_params=_params(("parallel",)),
        name="rmsnorm",
    )(x, g.reshape(1, d))


def _matmul_kernel(a_ref, w_ref, o_ref, *, act):
    acc = jnp.dot(a_ref[...], w_ref[...], preferred_element_type=F32)
    if act == "relu2":
        acc = jnp.maximum(acc, 0.0)
        acc = acc * acc
    o_ref[...] = acc.astype(o_ref.dtype)


def _matmul(a, w, out_dtype, act=None, tm_cap=1024, tn_cap=1024):
    m, k = a.shape
    n = w.shape[1]
    tm = _row_tile(m, tm_cap)
    tn = _row_tile(n, tn_cap)
    return pl.pallas_call(
        functools.partial(_matmul_kernel, act=act),
        out_shape=jax.ShapeDtypeStruct((m, n), out_dtype),
        grid=(m // tm, n // tn),
        in_specs=[pl.BlockSpec((tm, k), lambda i, j: (i, 0)), pl.BlockSpec((k, tn), lambda i, j: (0, j))],
        out_specs=pl.BlockSpec((tm, tn), lambda i, j: (i, j)),
        compiler_params=_params(("parallel", "parallel")),
        name="matmul",
    )(a, w)


def _matmul_t_kernel(wt_ref, a_ref, o_ref):
    o_ref[...] = lax.dot_general(wt_ref[...], a_ref[...], (((1,), (1,)), ((), ())),
                                 preferred_element_type=F32).astype(o_ref.dtype)


def _matmul_t(w_t, a, out_dtype):
    n, k = w_t.shape
    m = a.shape[0]
    tn = _row_tile(n, 1024)
    tm = _row_tile(m, 1024)
    return pl.pallas_call(
        _matmul_t_kernel,
        out_shape=jax.ShapeDtypeStruct((n, m), out_dtype),
        grid=(n // tn, m // tm),
        in_specs=[pl.BlockSpec((tn, k), lambda j, i: (j, 0)), pl.BlockSpec((tm, k), lambda j, i: (i, 0))],
        out_specs=pl.BlockSpec((tn, tm), lambda j, i: (j, i)),
        compiler_params=_params(("parallel", "parallel")),
        name="matmul_t",
    )(w_t, a)


def _matmul_res_kernel(a_ref, w_ref, r_ref, o_ref):
    o_ref[...] = r_ref[...] + jnp.dot(a_ref[...], w_ref[...], preferred_element_type=F32)


def _matmul_res(a, w, res):
    m, k = a.shape
    n = w.shape[1]
    tm = _row_tile(m, 512)
    return pl.pallas_call(
        _matmul_res_kernel,
        out_shape=jax.ShapeDtypeStruct((m, n), F32),
        grid=(m // tm,),
        in_specs=[pl.BlockSpec((tm, k), lambda i: (i, 0)), pl.BlockSpec((k, n), lambda i: (0, 0)),
                  pl.BlockSpec((tm, n), lambda i: (i, 0))],
        out_specs=pl.BlockSpec((tm, n), lambda i: (i, 0)),
        compiler_params=_params(("parallel",)),
        name="matmul_res",
    )(a, w, res)


def _matmul_res_norm_kernel(a_ref, w_ref, r_ref, g_ref, o_ref):
    x = r_ref[...] + jnp.dot(a_ref[...], w_ref[...], preferred_element_type=F32)
    ms = jnp.mean(x * x, axis=-1, keepdims=True)
    o_ref[...] = x * lax.rsqrt(ms + NORM_EPS) * g_ref[...]


def _matmul_res_norm(a, w, res, g):
    m, k = a.shape
    n = w.shape[1]
    tm = _row_tile(m, 512)
    return pl.pallas_call(
        _matmul_res_norm_kernel,
        out_shape=jax.ShapeDtypeStruct((m, n), F32),
        grid=(m // tm,),
        in_specs=[pl.BlockSpec((tm, k), lambda i: (i, 0)), pl.BlockSpec((k, n), lambda i: (0, 0)),
                  pl.BlockSpec((tm, n), lambda i: (i, 0)), pl.BlockSpec((1, n), lambda i: (0, 0))],
        out_specs=pl.BlockSpec((tm, n), lambda i: (i, 0)),
        compiler_params=_params(("parallel",)),
        name="matmul_res_norm",
    )(a, w, res, g.reshape(1, n))


def _merge_kernel(ys_ref, oa_ref, gl_ref, x_ref, wsp_ref, wap_ref, wo_ref, o_ref):
    d = o_ref.shape[-1]
    gates = _sigmoid(gl_ref[...])
    a = jnp.dot(ys_ref[...], wsp_ref[...], preferred_element_type=F32)
    b = jnp.dot(oa_ref[...], wap_ref[...], preferred_element_type=F32)
    merged = gates[:, :d] * a + gates[:, d:] * b
    o_ref[...] = x_ref[...] + jnp.dot(merged.astype(BF16), wo_ref[...], preferred_element_type=F32)


def _merge(y_ssm, o_attn, gate_logits, x, w_sp, w_ap, w_o):
    m, d = x.shape
    tm = _row_tile(m, 256)
    row = lambda i: (i, 0)
    full = lambda i: (0, 0)
    return pl.pallas_call(
        _merge_kernel,
        out_shape=jax.ShapeDtypeStruct((m, d), F32),
        grid=(m // tm,),
        in_specs=[pl.BlockSpec((tm, y_ssm.shape[1]), row), pl.BlockSpec((tm, o_attn.shape[1]), row),
                  pl.BlockSpec((tm, 2 * d), row), pl.BlockSpec((tm, d), row),
                  pl.BlockSpec(w_sp.shape, full), pl.BlockSpec(w_ap.shape, full), pl.BlockSpec(w_o.shape, full)],
        out_specs=pl.BlockSpec((tm, d), row),
        compiler_params=_params(("parallel",)),
        name="merge",
    )(y_ssm, o_attn, gate_logits, x, w_sp, w_ap, w_o)


def _cumsum_rows(x):
    t = x.shape[0]
    row = lax.broadcasted_iota(jnp.int32, x.shape, 0)
    k = 1
    while k < t:
        x = x + jnp.where(row >= k, pltpu.roll(x, k, axis=0), 0.0)
        k *= 2
    return x


def _ssd_kernel(z_ref, xbc_ref, dtp_ref, cs_ref, h0_ref, cw_ref, cb_ref, dtb_ref, alog_ref, dsk_ref, gn_ref,
                y_ref, nc_ref, h_ref, cbuf, acs_s, acst_s, dt_s, ybuf, *, t_in, t_c):
    c = pl.program_id(1)
    n_heads = h_ref.shape[1]
    hd = SSM_HEAD_DIM
    d_inner = n_heads * hd
    gn = SSM_GROUPS * SSM_STATE
    hpg = n_heads // SSM_GROUPS
    pad0 = SUBLANES
    tail = CONV_WIDTH - 1

    @pl.when(c == 0)
    def _():
        h_ref[...] = h0_ref[...]
        cbuf[...] = jnp.zeros(cbuf.shape, F32)
        cbuf[pad0 - tail:pad0, :] = cs_ref[0]

    cbuf[pad0:pad0 + t_in, :] = xbc_ref[0]
    acc = cb_ref[...]
    for j in range(CONV_WIDTH):
        acc = acc + cbuf[pad0 - tail + j:pad0 - tail + j + t_c, :] * cw_ref[j:j + 1, :]
    xbc = _silu(acc)
    new_tail = cbuf[pad0 + t_in - tail:pad0 + t_in, :]
    nc_ref[0] = new_tail
    cbuf[pad0 - tail:pad0, :] = new_tail

    rows = lax.broadcasted_iota(jnp.int32, (t_c, LANES), 0)
    dtp = dtp_ref[0]
    if t_c != t_in:
        dtp = jnp.concatenate([dtp, jnp.zeros((t_c - t_in, LANES), F32)], axis=0)
    pre = dtp + dtb_ref[...]
    dt = jnp.maximum(pre, 0.0) + jnp.log(1.0 + jnp.exp(-jnp.abs(pre)))
    dt = jnp.where(rows < t_in, dt, 0.0)
    a = -jnp.exp(alog_ref[...])
    acs = _cumsum_rows(dt * a)
    dt_s[...] = dt
    acs_s[...] = acs
    acst_s[...] = acs.T

    li = lax.broadcasted_iota(jnp.int32, (t_c, t_c), 0)
    si = lax.broadcasted_iota(jnp.int32, (t_c, t_c), 1)
    tril = li >= si
    a_last = acs_s[t_c - 1:t_c, :]

    for g in range(SSM_GROUPS):
        b_g = xbc[:, d_inner + g * SSM_STATE:d_inner + (g + 1) * SSM_STATE].astype(BF16)
        c_g = xbc[:, d_inner + gn + g * SSM_STATE:d_inner + gn + (g + 1) * SSM_STATE].astype(BF16)
        cb = lax.dot_general(c_g, b_g, (((1,), (1,)), ((), ())), preferred_element_type=F32)
        for r in range(hpg):
            h = g * hpg + r
            a_col = acs_s[:, h:h + 1]
            a_row = acst_s[h:h + 1, :]
            dt_col = dt_s[:, h:h + 1]
            decay = jnp.where(tril, jnp.exp(jnp.minimum(a_col - a_row, 0.0)), 0.0)
            x_h = xbc[:, h * hd:(h + 1) * hd]
            xd = x_h * dt_col
            y_diag = jnp.dot((cb * decay).astype(BF16), xd.astype(BF16), preferred_element_type=F32)
            s_h = h_ref[0, h]
            y_off = lax.dot_general(c_g, s_h.astype(BF16), (((1,), (1,)), ((), ())),
                                    preferred_element_type=F32)
            ybuf[:, h * hd:(h + 1) * hd] = y_diag + y_off * jnp.exp(a_col) + x_h * dsk_ref[:, h * hd:(h + 1) * hd]
            to_end = jnp.exp(a_last[:, h:h + 1] - a_col)
            upd = lax.dot_general((xd * to_end).astype(BF16), b_g, (((0,), (0,)), ((), ())),
                                  preferred_element_type=F32)
            h_ref[0, h] = s_h * jnp.exp(a_last[:, h:h + 1]) + upd

    z = z_ref[0]
    if t_c != t_in:
        z = jnp.concatenate([z, jnp.zeros((t_c - t_in, d_inner), F32)], axis=0)
    y = ybuf[...] * _silu(z)
    gw = d_inner // SSM_GROUPS
    parts = []
    for g in range(SSM_GROUPS):
        yg = y[:, g * gw:(g + 1) * gw]
        parts.append(yg * lax.rsqrt(jnp.mean(yg * yg, axis=-1, keepdims=True) + NORM_EPS))
    y = jnp.concatenate(parts, axis=-1) * gn_ref[...]
    y_ref[0] = y[:t_in].astype(y_ref.dtype)


def _ssd_branch(z, xbc, dtp, conv_state, h0, conv_w, conv_b, dt_bias, a_log, d_skip, g_ssm_norm):
    bsz, seq, d_inner = z.shape
    conv_dim = xbc.shape[-1]
    n_heads = h0.shape[1]
    t_in = min(SSD_CHUNK, seq)
    assert seq % t_in == 0
    t_c = -(-t_in // SUBLANES) * SUBLANES
    nchunk = seq // t_in
    pad_h = LANES - n_heads
    dtb = jnp.pad(dt_bias, (0, pad_h)).reshape(1, LANES)
    alog = jnp.pad(a_log, (0, pad_h)).reshape(1, LANES)
    dsk = jnp.repeat(d_skip, SSM_HEAD_DIM).reshape(1, d_inner)
    seq_blk = lambda w: pl.BlockSpec((1, t_in, w), lambda b, c: (b, c, 0))
    per_b3 = lambda s: pl.BlockSpec((1,) + s, lambda b, c: (b,) + (0,) * len(s))
    const2 = lambda s: pl.BlockSpec(s, lambda b, c: (0, 0))
    kern = functools.partial(_ssd_kernel, t_in=t_in, t_c=t_c)
    return pl.pallas_call(
        kern,
        out_shape=(jax.ShapeDtypeStruct((bsz, seq, d_inner), BF16),
                   jax.ShapeDtypeStruct((bsz, CONV_WIDTH - 1, conv_dim), F32),
                   jax.ShapeDtypeStruct(h0.shape, F32)),
        grid=(bsz, nchunk),
        in_specs=[seq_blk(d_inner), seq_blk(conv_dim), seq_blk(LANES),
                  per_b3((CONV_WIDTH - 1, conv_dim)), per_b3(h0.shape[1:]),
                  const2((CONV_WIDTH, conv_dim)), const2((1, conv_dim)), const2((1, LANES)), const2((1, LANES)),
                  const2((1, d_inner)), const2((1, d_inner))],
        out_specs=(seq_blk(d_inner), per_b3((CONV_WIDTH - 1, conv_dim)), per_b3(h0.shape[1:])),
        scratch_shapes=[pltpu.VMEM((SUBLANES + t_c, conv_dim), F32),
                        pltpu.VMEM((t_c, LANES), F32), pltpu.VMEM((LANES, t_c), F32), pltpu.VMEM((t_c, LANES), F32),
                        pltpu.VMEM((t_c, d_inner), F32)],
        compiler_params=_params(("parallel", "arbitrary")),
        name="ssd_branch",
    )(z, xbc, dtp, conv_state, h0, conv_w, conv_b.reshape(1, conv_dim), dtb, alog, dsk,
      g_ssm_norm.reshape(1, d_inner))


def _kmean_kernel(kt_ref, km_ref, kb_ref):
    i = pl.program_id(0)
    k = kt_ref[...].T
    km_ref[pl.ds(i, 1), :] = jnp.sum(k, axis=0, keepdims=True) * (1.0 / MOBA_BLOCK)
    kb_ref[...] = k.astype(BF16)


def _kmean(k_t):
    hd, seq = k_t.shape
    nb = seq // MOBA_BLOCK
    return pl.pallas_call(
        _kmean_kernel,
        out_shape=(jax.ShapeDtypeStruct((nb, hd), F32), jax.ShapeDtypeStruct((seq, hd), BF16)),
        grid=(nb,),
        in_specs=[pl.BlockSpec((hd, MOBA_BLOCK), lambda i: (0, i))],
        out_specs=(pl.BlockSpec((nb, hd), lambda i: (0, 0)), pl.BlockSpec((MOBA_BLOCK, hd), lambda i: (i, 0))),
        compiler_params=_params(("arbitrary",)),
        name="moba_kmean",
    )(k_t)


def _moba_prompt_kernel(qt_ref, k_ref, vt_ref, km_ref, offb_ref, ownb_ref, slope_ref, o_ref,
                        qm_s, sel_s, m_s, l_s, acc_s, z_s, zmax_s, p_s, alpha_s):
    i = pl.program_id(1)
    nb = km_ref.shape[0]
    blk = MOBA_BLOCK
    hd = ATTN_HEAD_DIM
    heads = qt_ref.shape[0] // hd
    scale = hd ** -0.5 * LOG2E
    lane_rows = lax.broadcasted_iota(jnp.int32, qt_ref.shape, 0)
    blk_ids = lax.broadcasted_iota(jnp.int32, (nb, blk), 0)

    for hh in range(heads):
        in_head = (lane_rows >= hh * hd) & (lane_rows < (hh + 1) * hd)
        qm = jnp.where(in_head, qt_ref[...], jnp.zeros_like(qt_ref[...]))
        qm_s[hh] = (qm.astype(F32) * scale).astype(BF16)
        gate = jnp.dot(km_ref[...], qm.astype(F32), preferred_element_type=F32,
                       precision=lax.Precision.HIGHEST)
        g = jnp.where(blk_ids < i, gate, NEG_INF)
        sel = jnp.zeros((nb, blk), jnp.bool_)
        for _ in range(MOBA_TOPK):
            top = jnp.max(g, axis=0, keepdims=True)
            first = jnp.min(jnp.where(g == top, blk_ids, nb), axis=0, keepdims=True)
            pick = blk_ids == first
            sel = sel | pick
            g = jnp.where(pick, NEG_INF, g)
        sel = sel & (blk_ids < i)
        sel_s[hh] = jnp.where(sel, 0.0, NEG_INF)

        s_t = jnp.dot(k_ref[pl.ds(i * blk, blk), :], qm_s[hh], preferred_element_type=F32) + ownb_ref[hh]
        m0 = jnp.max(s_t, axis=0, keepdims=True)
        p = jnp.exp2(s_t - m0)
        m_s[hh] = m0
        l_s[hh] = jnp.sum(p, axis=0, keepdims=True)
        acc_s[hh] = jnp.dot(vt_ref[hh, i], p.astype(BF16), preferred_element_type=F32)

    def scores(j, hh):
        z = jnp.dot(k_ref[pl.ds(j * blk, blk), :], qm_s[hh], preferred_element_type=F32) + offb_ref[hh]
        z_s[hh] = z
        zmax_s[hh] = jnp.max(z, axis=0, keepdims=True)

    def softmax(j, hh):
        dist = ((i - j) * blk).astype(F32)
        c_j = sel_s[hh, pl.ds(j, 1), :] - slope_ref[hh] * dist
        m_old = m_s[hh]
        m_new = jnp.maximum(m_old, zmax_s[hh] + c_j)
        alpha = jnp.exp2(m_old - m_new)
        p = jnp.exp2(z_s[hh] + (c_j - m_new))
        l_s[hh] = alpha * l_s[hh] + jnp.sum(p, axis=0, keepdims=True)
        p_s[hh] = p.astype(BF16)
        alpha_s[hh] = alpha
        m_s[hh] = m_new

    def accumulate(j, hh):
        acc_s[hh] = alpha_s[hh] * acc_s[hh] + jnp.dot(vt_ref[hh, j], p_s[hh], preferred_element_type=F32)

    last = heads - 1
    p_s[last] = jnp.zeros(p_s.shape[1:], BF16)
    alpha_s[last] = jnp.ones(alpha_s.shape[1:], F32)
    scores(0, 0)

    def body(j, carry):
        for hh in range(heads):
            nxt = (j, hh + 1) if hh + 1 < heads else (j + 1, 0)
            prv = (j, hh - 1) if hh > 0 else (jnp.maximum(j - 1, 0), last)
            scores(*nxt)
            accumulate(*prv)
            softmax(j, hh)
        return carry

    lax.fori_loop(0, i, body, 0)
    accumulate(jnp.maximum(i - 1, 0), last)

    for hh in range(heads):
        o_ref[hh * hd:(hh + 1) * hd, :] = (acc_s[hh] / l_s[hh]).astype(o_ref.dtype)


def _alibi_slopes(n_heads):
    return 2.0 ** (-8.0 * jnp.arange(1, n_heads + 1, dtype=F32) / n_heads)


def _moba_prompt(q_t, k_bf, v_t4, kmean, n_heads):
    hd = ATTN_HEAD_DIM
    blk = MOBA_BLOCK
    seq = k_bf.shape[0]
    nb = seq // blk
    hps = LANES // hd
    slopes = _alibi_slopes(n_heads) * LOG2E
    offs = jnp.arange(blk, dtype=F32)
    offb = jnp.broadcast_to((slopes[:, None] * offs[None, :])[:, :, None], (n_heads, blk, blk))
    causal = jnp.where(offs[:, None] <= offs[None, :], 0.0, NEG_INF)
    ownb = offb + causal[None]
    slope_rows = jnp.broadcast_to(slopes[:, None, None], (n_heads, 1, blk))
    return pl.pallas_call(
        _moba_prompt_kernel,
        out_shape=jax.ShapeDtypeStruct((n_heads * hd, seq), BF16),
        grid=(n_heads // hps, nb),
        in_specs=[pl.BlockSpec((hps * hd, blk), lambda g, i: (g, i)),
                  pl.BlockSpec((seq, hps * hd), lambda g, i: (0, g)),
                  pl.BlockSpec((hps, nb, hd, blk), lambda g, i: (g, 0, 0, 0)),
                  pl.BlockSpec((nb, hps * hd), lambda g, i: (0, g)),
                  pl.BlockSpec((hps, blk, blk), lambda g, i: (g, 0, 0)),
                  pl.BlockSpec((hps, blk, blk), lambda g, i: (g, 0, 0)),
                  pl.BlockSpec((hps, 1, blk), lambda g, i: (g, 0, 0))],
        out_specs=pl.BlockSpec((hps * hd, blk), lambda g, i: (g, i)),
        scratch_shapes=[pltpu.VMEM((hps, hps * hd, blk), BF16), pltpu.VMEM((hps, nb, blk), F32),
                        pltpu.VMEM((hps, 1, blk), F32), pltpu.VMEM((hps, 1, blk), F32),
                        pltpu.VMEM((hps, hd, blk), F32), pltpu.VMEM((hps, blk, blk), F32),
                        pltpu.VMEM((hps, 1, blk), F32), pltpu.VMEM((hps, blk, blk), BF16),
                        pltpu.VMEM((hps, 1, blk), F32)],
        compiler_params=_params(("parallel", "arbitrary")),
        name="moba_prompt",
    )(q_t, k_bf, v_t4, kmean, offb, ownb, slope_rows)


def _xattn_kernel(q_ref, mk_ref, mv_ref, o_ref):
    dh = q_ref.shape[-1] // XATTN_HEADS
    scale = dh ** -0.5
    q = q_ref[0]
    for h in range(XATTN_HEADS):
        sl = slice(h * dh, (h + 1) * dh)
        s = lax.dot_general(q[:, sl], mk_ref[0, :, sl].astype(BF16), (((1,), (1,)), ((), ())),
                            preferred_element_type=F32) * scale
        p = jnp.exp(s - jnp.max(s, axis=-1, keepdims=True))
        p = p / jnp.sum(p, axis=-1, keepdims=True)
        o_ref[0, :, sl] = jnp.dot(p.astype(BF16), mv_ref[0, :, sl].astype(BF16),
                                  preferred_element_type=F32).astype(o_ref.dtype)


def _xattn(q, mk, mv):
    bsz, seq, d = q.shape
    n_mem = mk.shape[1]
    tm = _row_tile(seq, 512)
    return pl.pallas_call(
        _xattn_kernel,
        out_shape=jax.ShapeDtypeStruct((bsz, seq, d), BF16),
        grid=(bsz, seq // tm),
        in_specs=[pl.BlockSpec((1, tm, d), lambda b, i: (b, i, 0)),
                  pl.BlockSpec((1, n_mem, d), lambda b, i: (b, 0, 0)),
                  pl.BlockSpec((1, n_mem, d), lambda b, i: (b, 0, 0))],
        out_specs=pl.BlockSpec((1, tm, d), lambda b, i: (b, i, 0)),
        compiler_params=_params(("parallel", "parallel")),
        name="xattn",
    )(q, mk, mv)


GATE_PAGE_CHUNK = 8


def _sample_gate_kernel(pt_ref, qe_ref, ck_hbm, idx_ref, pbuf, kmt_s, sem, *, lay, n_pages):
    b = pl.program_id(0)
    chunk = pbuf.shape[1]
    ppb = MOBA_BLOCK // PAGE_SIZE
    n_chunks = n_pages // chunk
    total_chunks = pl.num_programs(0) * n_chunks
    nblk = n_pages // ppb
    hd_all = kmt_s.shape[0]

    def chunk_copies(g, slot):
        return [pltpu.make_async_copy(ck_hbm.at[lay, pt_ref[g * chunk + k]], pbuf.at[slot, k], sem.at[slot])
                for k in range(chunk)]

    @pl.when(b == 0)
    def _():
        for cp in chunk_copies(0, 0):
            cp.start()

    kmt_s[...] = jnp.zeros(kmt_s.shape, F32)
    lane = lax.broadcasted_iota(jnp.int32, kmt_s.shape, 1)

    def body(c, carry):
        g = b * n_chunks + c
        slot = g % 2

        @pl.when(g + 1 < total_chunks)
        def _():
            for cp in chunk_copies(g + 1, 1 - slot):
                cp.start()

        for cp in chunk_copies(g, slot):
            cp.wait()
        for bb in range(chunk // ppb):
            y = pbuf[slot, bb * ppb]
            for p in range(1, ppb):
                y = y + pbuf[slot, bb * ppb + p]
            col = jnp.sum(y.reshape(hd_all, PAGE_SIZE), axis=-1, keepdims=True) * (1.0 / MOBA_BLOCK)
            blk = c * (chunk // ppb) + bb
            kmt_s[...] = jnp.where(lane == blk, col, kmt_s[...])
        return carry

    lax.fori_loop(0, n_chunks, body, 0)

    gate = jnp.dot(qe_ref[0], kmt_s[...], preferred_element_type=F32, precision=lax.Precision.HIGHEST)
    lane_g = lax.broadcasted_iota(jnp.int32, gate.shape, 1)
    gate = jnp.where(lane_g < nblk, gate, NEG_INF)
    out = jnp.zeros(gate.shape, jnp.int32)
    for s in range(MOBA_TOPK):
        top = jnp.max(gate, axis=-1, keepdims=True)
        first = jnp.min(jnp.where(gate == top, lane_g, LANES), axis=-1, keepdims=True)
        out = jnp.where(lane_g == s, first, out)
        gate = jnp.where(lane_g == first, NEG_INF, gate)
    idx_ref[0] = out


def _sample_gate(page_table, q_exp, ck_t, lay):
    bsz, rows, hd_all = q_exp.shape
    n_pages = page_table.shape[1]
    _, _, n_heads, hd, page = ck_t.shape
    chunk = GATE_PAGE_CHUNK
    assert n_pages % chunk == 0 and chunk % (MOBA_BLOCK // PAGE_SIZE) == 0
    assert n_pages // (MOBA_BLOCK // PAGE_SIZE) <= LANES
    kern = functools.partial(_sample_gate_kernel, lay=lay, n_pages=n_pages)
    return pl.pallas_call(
        kern,
        out_shape=jax.ShapeDtypeStruct((bsz, rows, LANES), jnp.int32),
        grid_spec=pltpu.PrefetchScalarGridSpec(
            num_scalar_prefetch=1,
            grid=(bsz,),
            in_specs=[pl.BlockSpec((1, rows, hd_all), lambda b, pt: (b, 0, 0)),
                      pl.BlockSpec(memory_space=pltpu.HBM)],
            out_specs=pl.BlockSpec((1, rows, LANES), lambda b, pt: (b, 0, 0)),
            scratch_shapes=[pltpu.VMEM((2, chunk, n_heads, hd, page), F32), pltpu.VMEM((hd_all, LANES), F32),
                            pltpu.SemaphoreType.DMA((2,))]),
        compiler_params=_params(("arbitrary",)),
        name="sample_gate",
    )(page_table.reshape(-1), q_exp, ck_t)


def _sample_attn_kernel(pt_ref, idx_ref, q_ref, kn_ref, vn_ref, slope_ref, ck_hbm, cv_hbm, o_ref,
                        kbuf, vbuf, sem, *, lay, n_tok, n_pages, past):
    step = pl.program_id(0)
    n_steps = pl.num_programs(0)
    n_heads = ck_hbm.shape[2]
    hd = ck_hbm.shape[3]
    ppb = MOBA_BLOCK // PAGE_SIZE
    n_seg = n_tok * MOBA_TOPK
    n_keys = n_seg * MOBA_BLOCK
    rows_pad = SUBLANES

    def block_index(st, seg):
        b, h = st // n_heads, st % n_heads
        t, s = seg // MOBA_TOPK, seg % MOBA_TOPK
        return idx_ref[((b * n_tok + t) * n_heads + h) * MOBA_TOPK + s]

    def page_copies(st, slot):
        b, h = st // n_heads, st % n_heads
        out = []
        for seg in range(n_seg):
            blk = block_index(st, seg)
            for pg in range(ppb):
                phys = pt_ref[b * n_pages + blk * ppb + pg]
                dst = pl.ds((seg * ppb + pg) * PAGE_SIZE, PAGE_SIZE)
                out.append(pltpu.make_async_copy(ck_hbm.at[lay, phys, h], kbuf.at[slot, :, dst], sem.at[0, slot]))
                out.append(pltpu.make_async_copy(cv_hbm.at[lay, phys, h], vbuf.at[slot, :, dst], sem.at[1, slot]))
        return out

    slot = step % 2

    @pl.when(step == 0)
    def _():
        for cp in page_copies(step, slot):
            cp.start()

    @pl.when(step + 1 < n_steps)
    def _():
        for cp in page_copies(step + 1, 1 - slot):
            cp.start()

    for cp in page_copies(step, slot):
        cp.wait()

    slope = slope_ref[0]
    scale = hd ** -0.5
    zpad = jnp.zeros((rows_pad - n_tok, hd), F32)
    q = (jnp.concatenate([q_ref[0, 0], zpad], axis=0) * scale).astype(BF16)
    kn = jnp.concatenate([kn_ref[0, 0], zpad], axis=0).astype(BF16)
    vn = jnp.concatenate([vn_ref[0, 0], zpad], axis=0).astype(BF16)

    s_g = jnp.dot(q, kbuf[slot].astype(BF16), preferred_element_type=F32)
    col = lax.broadcasted_iota(jnp.int32, (1, n_keys), 1)
    seg_of_col = col // MOBA_BLOCK
    blk_of_col = jnp.zeros((1, n_keys), jnp.int32)
    for seg in range(n_seg):
        blk_of_col = jnp.where(seg_of_col == seg, block_index(step, seg), blk_of_col)
    kpos = blk_of_col * MOBA_BLOCK + col % MOBA_BLOCK
    row = lax.broadcasted_iota(jnp.int32, (rows_pad, n_keys), 0)
    dist = (past + row - kpos).astype(F32)
    s_g = jnp.where(row == seg_of_col // MOBA_TOPK, s_g - slope[:, :1] * dist, NEG_INF)

    s_n = lax.dot_general(q, kn, (((1,), (1,)), ((), ())), preferred_element_type=F32)
    r_n = lax.broadcasted_iota(jnp.int32, s_n.shape, 0)
    c_n = lax.broadcasted_iota(jnp.int32, s_n.shape, 1)
    s_n = jnp.where((c_n <= r_n) & (c_n < n_tok), s_n - slope[:, :1] * (r_n - c_n).astype(F32), NEG_INF)

    m = jnp.maximum(jnp.max(s_g, axis=-1, keepdims=True), jnp.max(s_n, axis=-1, keepdims=True))
    p_g = jnp.exp(s_g - m)
    p_n = jnp.exp(s_n - m)
    den = jnp.sum(p_g, axis=-1, keepdims=True) + jnp.sum(p_n, axis=-1, keepdims=True)
    o = lax.dot_general(p_g.astype(BF16), vbuf[slot].astype(BF16), (((1,), (1,)), ((), ())),
                        preferred_element_type=F32)
    o = o + jnp.dot(p_n.astype(BF16), vn, preferred_element_type=F32)
    o_ref[0, 0] = (o / den)[:n_tok].astype(o_ref.dtype)


def _sample_attn(page_table, idx_flat, q_h, k_h, v_h, ck_t, cv_t, lay):
    bsz, n_heads, n_tok, hd = q_h.shape
    n_pages = page_table.shape[1]
    past = n_pages * PAGE_SIZE
    n_keys = n_tok * MOBA_TOPK * MOBA_BLOCK
    slopes = jnp.broadcast_to(_alibi_slopes(n_heads)[:, None, None], (n_heads, 1, LANES))
    tok = pl.BlockSpec((1, 1, n_tok, hd), lambda s, pt, ix: (s // n_heads, s % n_heads, 0, 0))
    kern = functools.partial(_sample_attn_kernel, lay=lay, n_tok=n_tok, n_pages=n_pages, past=past)
    return pl.pallas_call(
        kern,
        out_shape=jax.ShapeDtypeStruct((bsz, n_heads, n_tok, hd), BF16),
        grid_spec=pltpu.PrefetchScalarGridSpec(
            num_scalar_prefetch=2,
            grid=(bsz * n_heads,),
            in_specs=[tok, tok, tok,
                      pl.BlockSpec((1, 1, LANES), lambda s, pt, ix: (s % n_heads, 0, 0)),
                      pl.BlockSpec(memory_space=pltpu.HBM), pl.BlockSpec(memory_space=pltpu.HBM)],
            out_specs=tok,
            scratch_shapes=[pltpu.VMEM((2, hd, n_keys), F32), pltpu.VMEM((2, hd, n_keys), F32),
                            pltpu.SemaphoreType.DMA((2, 2))]),
        compiler_params=_params(("arbitrary",)),
        name="sample_attn",
    )(page_table.reshape(-1), idx_flat, q_h, k_h, v_h, slopes, ck_t, cv_t)


def _moba_sample(q, k_new, v_new, ck_t, cv_t, page_table, lay):
    bsz, n_tok, hd_all = q.shape
    n_heads = ck_t.shape[2]
    hd = ck_t.shape[3]
    n_pages = page_table.shape[1]
    ppb = MOBA_BLOCK // PAGE_SIZE
    assert n_pages % ppb == 0, "past pages inside the new tokens' own block are not supported"
    assert n_pages // ppb >= MOBA_TOPK and n_tok <= SUBLANES
    head_of_lane = jnp.arange(hd_all) // hd
    head_mask = (head_of_lane[None, :] == jnp.arange(n_heads)[:, None]).astype(F32)
    q_exp = (q[:, :, None, :] * head_mask[None, None]).reshape(bsz, n_tok * n_heads, hd_all)
    idx = _sample_gate(page_table, q_exp, ck_t, lay)[:, :, :MOBA_TOPK]
    heads_first = lambda a: a.reshape(bsz, n_tok, n_heads, hd).transpose(0, 2, 1, 3)
    o = _sample_attn(page_table, idx.reshape(-1), heads_first(q), heads_first(k_new), heads_first(v_new),
                     ck_t, cv_t, lay)
    return o.transpose(0, 2, 1, 3).reshape(bsz, n_tok, hd_all)


def _prompt_attn(xn, w, n_heads):
    seq = xn.shape[0]
    hd = ATTN_HEAD_DIM
    assert seq % MOBA_BLOCK == 0
    nb = seq // MOBA_BLOCK
    q_t = _matmul_t(w["w_q"].T, xn, BF16)
    k_t = _matmul_t(w["w_k"].T, xn, F32)
    v_t = _matmul_t(w["w_v"].T, xn, F32)
    kmean, k_bf = _kmean(k_t)
    v_t4 = v_t.astype(BF16).reshape(n_heads, hd, nb, MOBA_BLOCK).transpose(0, 2, 1, 3)
    o = _moba_prompt(q_t, k_bf, v_t4, kmean, n_heads).T
    rows_first = lambda a_t: a_t.reshape(n_heads, hd, seq).transpose(2, 0, 1)
    return o, rows_first(k_t), rows_first(v_t)


def _layer(x, attn_fn, conv_state, h0, mem_k, mem_v, w, g_final):
    bsz, seq, d = x.shape
    rows = bsz * seq
    x2 = x.reshape(rows, d)
    xn = _rmsnorm(x2, w["g_norm_mix"], BF16)
    z = _matmul(xn, w["w_z"], F32)
    xbc = _matmul(xn, w["w_xbc"], F32)
    dtp = _matmul(xn, w["w_dt"], F32)
    gate_logits = _matmul(xn, w["w_g"], F32)
    y_ssm, new_conv, new_h = _ssd_branch(
        z.reshape(bsz, seq, -1), xbc.reshape(bsz, seq, -1), dtp.reshape(bsz, seq, -1), conv_state, h0,
        w["conv_w"], w["conv_b"], w["dt_bias"], w["a_log"], w["d_skip"], w["g_ssm_norm"])
    o_attn, k, v = attn_fn(xn.reshape(bsz, seq, d))
    x2 = _merge(y_ssm.reshape(rows, -1), o_attn.reshape(rows, -1), gate_logits, x2,
                w["w_ssm_proj"], w["w_attn_proj"], w["w_out"])
    xq = _matmul(_rmsnorm(x2, w["g_norm_x"], BF16), w["w_xq"], BF16)
    o_x = _xattn(xq.reshape(bsz, seq, d), mem_k, mem_v)
    x2 = _matmul_res(o_x.reshape(rows, d), w["w_xo"], x2)
    hid = _matmul(_rmsnorm(x2, w["g_norm_ffn"], BF16), w["w_up"], BF16, act="relu2")
    if g_final is None:
        x2 = _matmul_res(hid, w["w_down"], x2)
    else:
        x2 = _matmul_res_norm(hid, w["w_down"], x2, g_final)
    return x2.reshape(bsz, seq, d), k, v, new_conv, new_h


def kernel(x_prompt, x_sample, state_conv, state_ssm, cache_k, cache_v, cache_mem_k, cache_mem_v, page_table, mem_prompt, g_norm_mix, w_in, conv_w, conv_b, dt_bias, a_log, d_skip, g_ssm_norm, w_ssm_proj, w_attn_proj, w_out, g_norm_x, g_mem, w_xq, w_xk, w_xv, w_xo, g_norm_ffn, w_up, w_down, g_final):
    depth = w_in.shape[0]
    bp, seq_p, d = x_prompt.shape
    bd, seq_d, _ = x_sample.shape
    n_heads, hd = cache_k.shape[3], cache_k.shape[4]
    attn_dim = n_heads * hd
    ssm_heads = dt_bias.shape[1]
    d_inner = ssm_heads * SSM_HEAD_DIM
    conv_dim = conv_w.shape[2]
    n_mem = mem_prompt.shape[1]
    xh, xd = cache_mem_k.shape[3], cache_mem_k.shape[4]
    z_off = 0
    xbc_off = z_off + d_inner
    dt_off = xbc_off + conv_dim
    q_off = dt_off + ssm_heads
    k_off = q_off + attn_dim
    v_off = k_off + attn_dim
    g_off = v_off + attn_dim

    ck_t = jnp.transpose(cache_k, (0, 1, 3, 4, 2))
    cv_t = jnp.transpose(cache_v, (0, 1, 3, 4, 2))
    xp, xs = x_prompt, x_sample
    outs = {n: [] for n in ("kp", "vp", "cp", "hp", "mkp", "mvp", "ks", "vs", "cs", "hs")}
    for l in range(depth):
        wl = w_in[l].astype(BF16)
        w = {
            "g_norm_mix": g_norm_mix[l], "g_norm_x": g_norm_x[l], "g_norm_ffn": g_norm_ffn[l],
            "w_z": wl[:, z_off:xbc_off], "w_xbc": wl[:, xbc_off:dt_off],
            "w_dt": jnp.pad(wl[:, dt_off:q_off], ((0, 0), (0, LANES - ssm_heads))),
            "w_q": wl[:, q_off:k_off], "w_k": wl[:, k_off:v_off], "w_v": wl[:, v_off:g_off], "w_g": wl[:, g_off:],
            "conv_w": conv_w[l], "conv_b": conv_b[l], "dt_bias": dt_bias[l], "a_log": a_log[l],
            "d_skip": d_skip[l], "g_ssm_norm": g_ssm_norm[l],
            "w_ssm_proj": w_ssm_proj[l].astype(BF16), "w_attn_proj": w_attn_proj[l].astype(BF16),
            "w_out": w_out[l].astype(BF16), "w_xq": w_xq[l].astype(BF16), "w_xo": w_xo[l].astype(BF16),
            "w_up": w_up[l].astype(BF16), "w_down": w_down[l].astype(BF16),
        }
        last = g_final if l == depth - 1 else None

        mem_n = _rmsnorm(mem_prompt.reshape(bp * n_mem, d), g_mem[l], BF16)
        mk_p = _matmul(mem_n, w_xk[l].astype(BF16), F32).reshape(bp, n_mem, d)
        mv_p = _matmul(mem_n, w_xv[l].astype(BF16), F32).reshape(bp, n_mem, d)
        conv0 = jnp.zeros((bp, CONV_WIDTH - 1, conv_dim), F32)
        h0 = jnp.zeros((bp, ssm_heads, SSM_HEAD_DIM, SSM_STATE), F32)

        def attn_p(xn):
            per_seq = [_prompt_attn(xn[b], w, n_heads) for b in range(bp)]
            return tuple(jnp.stack(parts) for parts in zip(*per_seq))

        xp, kp, vp, cp, hp = _layer(xp, attn_p, conv0, h0, mk_p, mv_p, w, last)

        def attn_s(xn):
            proj = lambda name: _matmul(xn.reshape(bd * seq_d, d), w[name], F32).reshape(bd, seq_d, attn_dim)
            q, k, v = proj("w_q"), proj("w_k"), proj("w_v")
            o = _moba_sample(q, k, v, ck_t, cv_t, page_table, l)
            split = lambda a: a.reshape(bd, seq_d, n_heads, hd)
            return o, split(k), split(v)

        xs, ks, vs, cs, hs = _layer(xs, attn_s, state_conv[l], state_ssm[l],
                                    cache_mem_k[l].reshape(bd, n_mem, d), cache_mem_v[l].reshape(bd, n_mem, d),
                                    w, last)

        outs["kp"].append(kp)
        outs["vp"].append(vp)
        outs["cp"].append(cp)
        outs["hp"].append(hp)
        outs["mkp"].append(mk_p.reshape(bp, n_mem, xh, xd))
        outs["mvp"].append(mv_p.reshape(bp, n_mem, xh, xd))
        outs["ks"].append(ks)
        outs["vs"].append(vs)
        outs["cs"].append(cs)
        outs["hs"].append(hs)

    if depth == 0:
        raise ValueError("at least one layer is required")
    st = lambda n: jnp.stack(outs[n])
    return (xp, xs, st("kp"), st("vp"), st("cp"), st("hp"), st("mkp"), st("mvp"),
            st("ks"), st("vs"), st("cs"), st("hs"))
```

```python
import functools
import math

import jax
import jax.numpy as jnp
from jax import lax
from jax.experimental import pallas as pl
from jax.experimental.pallas import tpu as pltpu

F32 = jnp.float32
BF16 = jnp.bfloat16

NORM_EPS = 1e-6
SSM_HEAD_DIM = 64
SSM_GROUPS = 4
SSM_STATE = 128
CONV_WIDTH = 4
SSD_CHUNK = 128
ATTN_HEAD_DIM = 64
MOBA_BLOCK = 256
MOBA_TOPK = 3
PAGE_SIZE = 128
XATTN_HEADS = 4

LANES = 128
SUBLANES = 8
VMEM_LIMIT = 56 * 1024 * 1024
NEG_INF = float("-inf")
LOG2E = math.log2(math.e)
MATMUL_TILE = 1024


def _params(sem):
    return pltpu.CompilerParams(dimension_semantics=sem, vmem_limit_bytes=VMEM_LIMIT)


def _sigmoid(x):
    return 1.0 / (1.0 + jnp.exp(-x))


def _silu(x):
    return x * _sigmoid(x)


def _row_tile(m, cap):
    t = min(m, cap)
    while m % t:
        t //= 2
    return t


def _rmsnorm_kernel(x_ref, g_ref, o_ref):
    x = x_ref[...]
    ms = jnp.mean(x * x, axis=-1, keepdims=True)
    o_ref[...] = (x * lax.rsqrt(ms + NORM_EPS) * g_ref[...]).astype(o_ref.dtype)


def _rmsnorm(x, g, out_dtype):
    m, d = x.shape
    tm = _row_tile(m, 1024)
    return pl.pallas_call(
        _rmsnorm_kernel,
        out_shape=jax.ShapeDtypeStruct((m, d), out_dtype),
        grid=(m // tm,),
        in_specs=[pl.BlockSpec((tm, d), lambda i: (i, 0)), pl.BlockSpec((1, d), lambda i: (0, 0))],
        out_specs=pl.BlockSpec((tm, d), lambda i: (i, 0)),
        compiler_params=_params(("parallel",)),
        name="rmsnorm",
    )(x, g.reshape(1, d))


def _matmul_kernel(a_ref, w_ref, o_ref):
    o_ref[...] = jnp.dot(a_ref[...], w_ref[...], preferred_element_type=F32).astype(o_ref.dtype)


def _matmul(a, w, out_dtype):
    m, k = a.shape
    n = w.shape[1]
    tm = _row_tile(m, MATMUL_TILE)
    tn = _row_tile(n, MATMUL_TILE)
    return pl.pallas_call(
        _matmul_kernel,
        out_shape=jax.ShapeDtypeStruct((m, n), out_dtype),
        grid=(m // tm, n // tn),
        in_specs=[pl.BlockSpec((tm, k), lambda i, j: (i, 0)), pl.BlockSpec((k, tn), lambda i, j: (0, j))],
        out_specs=pl.BlockSpec((tm, tn), lambda i, j: (i, j)),
        compiler_params=_params(("parallel", "parallel")),
        name="matmul",
    )(a, w)


def _matmul_t_kernel(wt_ref, a_ref, o_ref):
    o_ref[...] = lax.dot_general(wt_ref[...], a_ref[...], (((1,), (1,)), ((), ())),
                                 preferred_element_type=F32).astype(o_ref.dtype)


def _matmul_t(w_t, a, out_dtype):
    n, k = w_t.shape
    m = a.shape[0]
    tn = _row_tile(n, MATMUL_TILE)
    tm = _row_tile(m, MATMUL_TILE)
    return pl.pallas_call(
        _matmul_t_kernel,
        out_shape=jax.ShapeDtypeStruct((n, m), out_dtype),
        grid=(n // tn, m // tm),
        in_specs=[pl.BlockSpec((tn, k), lambda j, i: (j, 0)), pl.BlockSpec((tm, k), lambda j, i: (i, 0))],
        out_specs=pl.BlockSpec((tn, tm), lambda j, i: (j, i)),
        compiler_params=_params(("parallel", "parallel")),
        name="matmul_t",
    )(w_t, a)


def _normed(x, g):
    return x * lax.rsqrt(jnp.mean(x * x, axis=-1, keepdims=True) + NORM_EPS) * g


def _matmul_res_kernel(a_ref, w_ref, r_ref, g_ref, o_ref, on_ref):
    x = r_ref[...] + jnp.dot(a_ref[...], w_ref[...], preferred_element_type=F32)
    o_ref[...] = x
    on_ref[...] = _normed(x, g_ref[...]).astype(on_ref.dtype)


def _matmul_res(a, w, res, g_next):
    m, k = a.shape
    n = w.shape[1]
    tm = _row_tile(m, 512)
    row = lambda i: (i, 0)
    return pl.pallas_call(
        _matmul_res_kernel,
        out_shape=(jax.ShapeDtypeStruct((m, n), F32), jax.ShapeDtypeStruct((m, n), BF16)),
        grid=(m // tm,),
        in_specs=[pl.BlockSpec((tm, k), row), pl.BlockSpec((k, n), lambda i: (0, 0)),
                  pl.BlockSpec((tm, n), row), pl.BlockSpec((1, n), lambda i: (0, 0))],
        out_specs=(pl.BlockSpec((tm, n), row), pl.BlockSpec((tm, n), row)),
        compiler_params=_params(("parallel",)),
        name="matmul_res",
    )(a, w, res, g_next.reshape(1, n))


def _ffn_kernel(xn_ref, x_ref, wu_ref, wd_ref, g_ref, o_ref, *, final_norm):
    hid = jnp.maximum(jnp.dot(xn_ref[...], wu_ref[...], preferred_element_type=F32), 0.0)
    x = x_ref[...] + jnp.dot((hid * hid).astype(BF16), wd_ref[...], preferred_element_type=F32)
    o_ref[...] = _normed(x, g_ref[...]) if final_norm else x


def _ffn(xn, x, w_up, w_down, g_final):
    m, d = x.shape
    d_ff = w_up.shape[1]
    tm = _row_tile(m, 256)
    row = lambda i: (i, 0)
    full = lambda i: (0, 0)
    g = jnp.ones((d,), F32) if g_final is None else g_final
    return pl.pallas_call(
        functools.partial(_ffn_kernel, final_norm=g_final is not None),
        out_shape=jax.ShapeDtypeStruct((m, d), F32),
        grid=(m // tm,),
        in_specs=[pl.BlockSpec((tm, d), row), pl.BlockSpec((tm, d), row), pl.BlockSpec((d, d_ff), full),
                  pl.BlockSpec((d_ff, d), full), pl.BlockSpec((1, d), full)],
        out_specs=pl.BlockSpec((tm, d), row),
        compiler_params=_params(("parallel",)),
        name="ffn",
    )(xn, x, w_up, w_down, g.reshape(1, d))


def _merge_kernel(ys_ref, oa_ref, gl_ref, x_ref, wsp_ref, wap_ref, wo_ref, g_ref, o_ref, on_ref):
    d = o_ref.shape[-1]
    gates = _sigmoid(gl_ref[...])
    a = jnp.dot(ys_ref[...], wsp_ref[...], preferred_element_type=F32)
    b = jnp.dot(oa_ref[...], wap_ref[...], preferred_element_type=F32)
    merged = gates[:, :d] * a + gates[:, d:] * b
    x = x_ref[...] + jnp.dot(merged.astype(BF16), wo_ref[...], preferred_element_type=F32)
    o_ref[...] = x
    on_ref[...] = _normed(x, g_ref[...]).astype(on_ref.dtype)


def _merge(y_ssm, o_attn, gate_logits, x, w_sp, w_ap, w_o, g_next):
    m, d = x.shape
    tm = _row_tile(m, 256)
    row = lambda i: (i, 0)
    full = lambda i: (0, 0)
    return pl.pallas_call(
        _merge_kernel,
        out_shape=(jax.ShapeDtypeStruct((m, d), F32), jax.ShapeDtypeStruct((m, d), BF16)),
        grid=(m // tm,),
        in_specs=[pl.BlockSpec((tm, y_ssm.shape[1]), row), pl.BlockSpec((tm, o_attn.shape[1]), row),
                  pl.BlockSpec((tm, 2 * d), row), pl.BlockSpec((tm, d), row),
                  pl.BlockSpec(w_sp.shape, full), pl.BlockSpec(w_ap.shape, full), pl.BlockSpec(w_o.shape, full),
                  pl.BlockSpec((1, d), full)],
        out_specs=(pl.BlockSpec((tm, d), row), pl.BlockSpec((tm, d), row)),
        compiler_params=_params(("parallel",)),
        name="merge",
    )(y_ssm, o_attn, gate_logits, x, w_sp, w_ap, w_o, g_next.reshape(1, d))


def _cumsum_rows(x):
    t = x.shape[0]
    row = lax.broadcasted_iota(jnp.int32, x.shape, 0)
    k = 1
    while k < t:
        x = x + jnp.where(row >= k, pltpu.roll(x, k, axis=0), 0.0)
        k *= 2
    return x


def _ssd_kernel(z_ref, xbc_ref, dtp_ref, cs_ref, h0_ref, cw_ref, cb_ref, dtb_ref, alog_ref, dsk_ref, gn_ref,
                y_ref, nc_ref, st_ref, cbuf, acs_s, acst_s, dtt_s, ybuf, *, t_in, t_c):
    c = pl.program_id(1)
    n_pairs = st_ref.shape[1]
    n_heads = 2 * n_pairs
    hd = SSM_HEAD_DIM
    d_inner = n_heads * hd
    n_state = SSM_STATE
    gn = SSM_GROUPS * n_state
    hpg = n_heads // SSM_GROUPS
    pad0 = SUBLANES
    tail = CONV_WIDTH - 1
    merged = t_c % LANES == 0

    @pl.when(c == 0)
    def _():
        st_ref[...] = h0_ref[...]
        cbuf[...] = jnp.zeros(cbuf.shape, F32)
        cbuf[pad0 - tail:pad0, :] = cs_ref[0]

    cbuf[pad0:pad0 + t_in, :] = xbc_ref[0]
    acc = cb_ref[...]
    for j in range(CONV_WIDTH):
        acc = acc + cbuf[pad0 - tail + j:pad0 - tail + j + t_c, :] * cw_ref[j:j + 1, :]
    xbc = _silu(acc)
    new_tail = cbuf[pad0 + t_in - tail:pad0 + t_in, :]
    nc_ref[0] = new_tail
    cbuf[pad0 - tail:pad0, :] = new_tail

    rows = lax.broadcasted_iota(jnp.int32, (t_c, LANES), 0)
    dtp = dtp_ref[0]
    if t_c != t_in:
        dtp = jnp.concatenate([dtp, jnp.zeros((t_c - t_in, LANES), F32)], axis=0)
    pre = dtp + dtb_ref[...]
    dt = jnp.maximum(pre, 0.0) + jnp.log(1.0 + jnp.exp(-jnp.abs(pre)))
    dt = jnp.where(rows < t_in, dt, 0.0)
    a = -jnp.exp(alog_ref[...])
    acs = _cumsum_rows(dt * a)
    acs_s[...] = acs
    acst_s[...] = acs.T
    dtt_s[...] = dt.T

    li = lax.broadcasted_iota(jnp.int32, (t_c, t_c), 0)
    si = lax.broadcasted_iota(jnp.int32, (t_c, t_c), 1)
    tril = li >= si
    low_half = lax.broadcasted_iota(jnp.int32, (1, 2 * hd), 1) < hd
    xs = xbc[:, :d_inner]
    xs_bf = xs.astype(BF16)

    for g in range(SSM_GROUPS):
        b_g = xbc[:, d_inner + g * n_state:d_inner + (g + 1) * n_state]
        c_g = xbc[:, d_inner + gn + g * n_state:d_inner + gn + (g + 1) * n_state]
        cb = lax.dot_general(c_g.astype(BF16), b_g.astype(BF16), (((1,), (1,)), ((), ())),
                             preferred_element_type=F32)
        b_gt = b_g.T
        for k in range(hpg // 2):
            pair = g * (hpg // 2) + k
            x_pair = xs_bf[:, pair * 2 * hd:(pair + 1) * 2 * hd]
            st_pair = st_ref[0, pair]
            st_bf = st_pair.astype(BF16)
            y_halves, upd_halves, keep_halves = [], [], []
            for e in range(2):
                h = 2 * pair + e
                a_col = jnp.broadcast_to(acs_s[:, h:h + 1], (t_c, LANES))
                a_row = acst_s[h:h + 1, :]
                dt_row = dtt_s[h:h + 1, :]
                a_end = acst_s[h:h + 1, t_c - 1:t_c]
                decay = jnp.where(tril, jnp.exp(jnp.minimum(a_col[:, :t_c] - a_row, 0.0)), 0.0)
                w_in = cb * decay * dt_row
                w_st = c_g * jnp.exp(a_col)
                if merged:
                    y_h = jnp.dot(jnp.concatenate([w_in, w_st], axis=1).astype(BF16),
                                  jnp.concatenate([x_pair, st_bf], axis=0), preferred_element_type=F32)
                else:
                    y_h = (jnp.dot(w_in.astype(BF16), x_pair, preferred_element_type=F32)
                           + jnp.dot(w_st.astype(BF16), st_bf, preferred_element_type=F32))
                w_end = dt_row * jnp.exp(a_end - a_row)
                upd_halves.append(jnp.dot((b_gt * w_end).astype(BF16), x_pair, preferred_element_type=F32))
                y_halves.append(y_h)
                keep_halves.append(jnp.exp(a_end))
            ybuf[:, pair * 2 * hd:(pair + 1) * 2 * hd] = jnp.where(low_half, y_halves[0], y_halves[1])
            st_ref[0, pair] = (st_pair * jnp.where(low_half, keep_halves[0], keep_halves[1])
                               + jnp.where(low_half, upd_halves[0], upd_halves[1]))

    z = z_ref[0]
    if t_c != t_in:
        z = jnp.concatenate([z, jnp.zeros((t_c - t_in, d_inner), F32)], axis=0)
    y = (ybuf[...] + xs * dsk_ref[...]) * _silu(z)
    gw = d_inner // SSM_GROUPS
    parts = []
    for g in range(SSM_GROUPS):
        yg = y[:, g * gw:(g + 1) * gw]
        parts.append(yg * lax.rsqrt(jnp.mean(yg * yg, axis=-1, keepdims=True) + NORM_EPS))
    y = jnp.concatenate(parts, axis=-1) * gn_ref[...]
    y_ref[0] = y[:t_in].astype(y_ref.dtype)


def _ssd_branch(z, xbc, dtp, conv_state, h0, conv_w, conv_b, dt_bias, a_log, d_skip, g_ssm_norm):
    bsz, seq, d_inner = z.shape
    conv_dim = xbc.shape[-1]
    n_heads, hd, n_state = h0.shape[1:]
    assert n_heads % (2 * SSM_GROUPS) == 0 and 2 * hd == LANES
    t_in = min(SSD_CHUNK, seq)
    assert seq % t_in == 0
    t_c = -(-t_in // SUBLANES) * SUBLANES
    nchunk = seq // t_in
    pad_h = LANES - n_heads
    dtb = jnp.pad(dt_bias, (0, pad_h)).reshape(1, LANES)
    alog = jnp.pad(a_log, (0, pad_h)).reshape(1, LANES)
    dsk = jnp.repeat(d_skip, SSM_HEAD_DIM).reshape(1, d_inner)
    pair_shape = (n_heads // 2, n_state, 2 * hd)
    st0 = h0.reshape(bsz, n_heads // 2, 2, hd, n_state).transpose(0, 1, 4, 2, 3).reshape((bsz,) + pair_shape)
    seq_blk = lambda w: pl.BlockSpec((1, t_in, w), lambda b, c: (b, c, 0))
    per_b3 = lambda s: pl.BlockSpec((1,) + s, lambda b, c: (b,) + (0,) * len(s))
    const2 = lambda s: pl.BlockSpec(s, lambda b, c: (0, 0))
    kern = functools.partial(_ssd_kernel, t_in=t_in, t_c=t_c)
    y, new_conv, st = pl.pallas_call(
        kern,
        out_shape=(jax.ShapeDtypeStruct((bsz, seq, d_inner), BF16),
                   jax.ShapeDtypeStruct((bsz, CONV_WIDTH - 1, conv_dim), F32),
                   jax.ShapeDtypeStruct((bsz,) + pair_shape, F32)),
        grid=(bsz, nchunk),
        in_specs=[seq_blk(d_inner), seq_blk(conv_dim), seq_blk(LANES),
                  per_b3((CONV_WIDTH - 1, conv_dim)), per_b3(pair_shape),
                  const2((CONV_WIDTH, conv_dim)), const2((1, conv_dim)), const2((1, LANES)), const2((1, LANES)),
                  const2((1, d_inner)), const2((1, d_inner))],
        out_specs=(seq_blk(d_inner), per_b3((CONV_WIDTH - 1, conv_dim)), per_b3(pair_shape)),
        scratch_shapes=[pltpu.VMEM((SUBLANES + t_c, conv_dim), F32),
                        pltpu.VMEM((t_c, LANES), F32), pltpu.VMEM((LANES, t_c), F32), pltpu.VMEM((LANES, t_c), F32),
                        pltpu.VMEM((t_c, d_inner), F32)],
        compiler_params=_params(("parallel", "arbitrary")),
        name="ssd_branch",
    )(z, xbc, dtp, conv_state, st0, conv_w, conv_b.reshape(1, conv_dim), dtb, alog, dsk,
      g_ssm_norm.reshape(1, d_inner))
    h_new = st.reshape(bsz, n_heads // 2, n_state, 2, hd).transpose(0, 1, 3, 4, 2).reshape(h0.shape)
    return y, new_conv, h_new


def _kmean_kernel(kt_ref, km_ref, kb_ref):
    i = pl.program_id(0)
    k = kt_ref[...].T
    km_ref[pl.ds(i, 1), :] = jnp.sum(k, axis=0, keepdims=True) * (1.0 / MOBA_BLOCK)
    kb_ref[...] = k.astype(BF16)


def _kmean(k_t):
    hd, seq = k_t.shape
    nb = seq // MOBA_BLOCK
    return pl.pallas_call(
        _kmean_kernel,
        out_shape=(jax.ShapeDtypeStruct((nb, hd), F32), jax.ShapeDtypeStruct((seq, hd), BF16)),
        grid=(nb,),
        in_specs=[pl.BlockSpec((hd, MOBA_BLOCK), lambda i: (0, i))],
        out_specs=(pl.BlockSpec((nb, hd), lambda i: (0, 0)), pl.BlockSpec((MOBA_BLOCK, hd), lambda i: (i, 0))),
        compiler_params=_params(("arbitrary",)),
        name="moba_kmean",
    )(k_t)


MOBA_HEADS_PER_STEP = 4


def _moba_prompt_kernel(qt_ref, k_ref, vt_ref, km_ref, offb_ref, ownb_ref, slope_ref, o_ref,
                        qm_s, sel_s, m_s, l_s, acc_s, z_s, zmax_s, p_s, alpha_s):
    i = pl.program_id(1)
    nb = km_ref.shape[0]
    blk = MOBA_BLOCK
    hd = ATTN_HEAD_DIM
    heads = qt_ref.shape[0] // hd
    scale = hd ** -0.5 * LOG2E
    lane_rows = lax.broadcasted_iota(jnp.int32, qt_ref.shape, 0)
    blk_ids = lax.broadcasted_iota(jnp.int32, (nb, blk), 0)

    for hh in range(heads):
        in_head = (lane_rows >= hh * hd) & (lane_rows < (hh + 1) * hd)
        qm = jnp.where(in_head, qt_ref[...], jnp.zeros_like(qt_ref[...]))
        qm_s[hh] = (qm.astype(F32) * scale).astype(BF16)
        gate = jnp.dot(km_ref[...], qm.astype(F32), preferred_element_type=F32,
                       precision=lax.Precision.HIGHEST)
        g = jnp.where(blk_ids < i, gate, NEG_INF)
        sel = jnp.zeros((nb, blk), jnp.bool_)
        for _ in range(MOBA_TOPK):
            top = jnp.max(g, axis=0, keepdims=True)
            first = jnp.min(jnp.where(g == top, blk_ids, nb), axis=0, keepdims=True)
            pick = blk_ids == first
            sel = sel | pick
            g = jnp.where(pick, NEG_INF, g)
        sel = sel & (blk_ids < i)
        sel_s[hh] = jnp.where(sel, 0.0, NEG_INF)

        s_t = jnp.dot(k_ref[pl.ds(i * blk, blk), :], qm_s[hh], preferred_element_type=F32) + ownb_ref[hh]
        m0 = jnp.max(s_t, axis=0, keepdims=True)
        p = jnp.exp2(s_t - m0)
        m_s[hh] = m0
        l_s[hh] = jnp.sum(p, axis=0, keepdims=True)
        acc_s[hh] = jnp.dot(vt_ref[hh, i], p.astype(BF16), preferred_element_type=F32)

    def scores(j, hh):
        z = jnp.dot(k_ref[pl.ds(j * blk, blk), :], qm_s[hh], preferred_element_type=F32) + offb_ref[hh]
        z_s[hh] = z
        zmax_s[hh] = jnp.max(z, axis=0, keepdims=True)

    def softmax(j, hh):
        dist = ((i - j) * blk).astype(F32)
        c_j = sel_s[hh, pl.ds(j, 1), :] - slope_ref[hh] * dist
        m_old = m_s[hh]
        m_new = jnp.maximum(m_old, zmax_s[hh] + c_j)
        alpha = jnp.exp2(m_old - m_new)
        p = jnp.exp2(z_s[hh] + (c_j - m_new))
        l_s[hh] = alpha * l_s[hh] + jnp.sum(p, axis=0, keepdims=True)
        p_s[hh] = p.astype(BF16)
        alpha_s[hh] = alpha
        m_s[hh] = m_new

    def accumulate(j, hh):
        acc_s[hh] = alpha_s[hh] * acc_s[hh] + jnp.dot(vt_ref[hh, j], p_s[hh], preferred_element_type=F32)

    for hh in range(heads):
        p_s[hh] = jnp.zeros(p_s.shape[1:], BF16)
        alpha_s[hh] = jnp.ones(alpha_s.shape[1:], F32)
        scores(0, hh)

    def body(j, carry):
        for hh in range(heads):
            accumulate(jnp.maximum(j - 1, 0), hh)
            softmax(j, hh)
            scores(j + 1, hh)
        return carry

    lax.fori_loop(0, i, body, 0)
    for hh in range(heads):
        accumulate(jnp.maximum(i - 1, 0), hh)

    for hh in range(heads):
        o_ref[hh * hd:(hh + 1) * hd, :] = (acc_s[hh] / l_s[hh]).astype(o_ref.dtype)


def _alibi_slopes(n_heads):
    return 2.0 ** (-8.0 * jnp.arange(1, n_heads + 1, dtype=F32) / n_heads)


def _moba_prompt(q_t, k_bf, v_t4, kmean, n_heads):
    hd = ATTN_HEAD_DIM
    blk = MOBA_BLOCK
    seq = k_bf.shape[0]
    nb = seq // blk
    hps = MOBA_HEADS_PER_STEP
    slopes = _alibi_slopes(n_heads) * LOG2E
    offs = jnp.arange(blk, dtype=F32)
    offb = jnp.broadcast_to((slopes[:, None] * offs[None, :])[:, :, None], (n_heads, blk, blk))
    causal = jnp.where(offs[:, None] <= offs[None, :], 0.0, NEG_INF)
    ownb = offb + causal[None]
    slope_rows = jnp.broadcast_to(slopes[:, None, None], (n_heads, 1, blk))
    return pl.pallas_call(
        _moba_prompt_kernel,
        out_shape=jax.ShapeDtypeStruct((n_heads * hd, seq), BF16),
        grid=(n_heads // hps, nb),
        in_specs=[pl.BlockSpec((hps * hd, blk), lambda g, i: (g, i)),
                  pl.BlockSpec((seq, hps * hd), lambda g, i: (0, g)),
                  pl.BlockSpec((hps, nb, hd, blk), lambda g, i: (g, 0, 0, 0)),
                  pl.BlockSpec((nb, hps * hd), lambda g, i: (0, g)),
                  pl.BlockSpec((hps, blk, blk), lambda g, i: (g, 0, 0)),
                  pl.BlockSpec((hps, blk, blk), lambda g, i: (g, 0, 0)),
                  pl.BlockSpec((hps, 1, blk), lambda g, i: (g, 0, 0))],
        out_specs=pl.BlockSpec((hps * hd, blk), lambda g, i: (g, i)),
        scratch_shapes=[pltpu.VMEM((hps, hps * hd, blk), BF16), pltpu.VMEM((hps, nb, blk), F32),
                        pltpu.VMEM((hps, 1, blk), F32), pltpu.VMEM((hps, 1, blk), F32),
                        pltpu.VMEM((hps, hd, blk), F32), pltpu.VMEM((hps, blk, blk), F32),
                        pltpu.VMEM((hps, 1, blk), F32), pltpu.VMEM((hps, blk, blk), BF16),
                        pltpu.VMEM((hps, 1, blk), F32)],
        compiler_params=_params(("parallel", "arbitrary")),
        name="moba_prompt",
    )(q_t, k_bf, v_t4, kmean, offb, ownb, slope_rows)


def _xattn_kernel(q_ref, mk_ref, mv_ref, o_ref):
    dh = q_ref.shape[-1] // XATTN_HEADS
    scale = dh ** -0.5
    q = q_ref[0]
    for h in range(XATTN_HEADS):
        sl = slice(h * dh, (h + 1) * dh)
        s = lax.dot_general(q[:, sl], mk_ref[0, :, sl].astype(BF16), (((1,), (1,)), ((), ())),
                            preferred_element_type=F32) * scale
        p = jnp.exp(s - jnp.max(s, axis=-1, keepdims=True))
        p = p / jnp.sum(p, axis=-1, keepdims=True)
        o_ref[0, :, sl] = jnp.dot(p.astype(BF16), mv_ref[0, :, sl].astype(BF16),
                                  preferred_element_type=F32).astype(o_ref.dtype)


def _xattn(q, mk, mv):
    bsz, seq, d = q.shape
    n_mem = mk.shape[1]
    tm = _row_tile(seq, 512)
    return pl.pallas_call(
        _xattn_kernel,
        out_shape=jax.ShapeDtypeStruct((bsz, seq, d), BF16),
        grid=(bsz, seq // tm),
        in_specs=[pl.BlockSpec((1, tm, d), lambda b, i: (b, i, 0)),
                  pl.BlockSpec((1, n_mem, d), lambda b, i: (b, 0, 0)),
                  pl.BlockSpec((1, n_mem, d), lambda b, i: (b, 0, 0))],
        out_specs=pl.BlockSpec((1, tm, d), lambda b, i: (b, i, 0)),
        compiler_params=_params(("parallel", "parallel")),
        name="xattn",
    )(q, mk, mv)


GATE_PAGE_CHUNK = 8


def _sample_gate_kernel(pt_ref, qe_ref, ck_hbm, idx_ref, pbuf, kmt_s, sem, *, lay, n_pages):
    b = pl.program_id(0)
    chunk = pbuf.shape[1]
    ppb = MOBA_BLOCK // PAGE_SIZE
    n_chunks = n_pages // chunk
    total_chunks = pl.num_programs(0) * n_chunks
    nblk = n_pages // ppb
    hd_all = kmt_s.shape[0]

    def chunk_copies(g, slot):
        return [pltpu.make_async_copy(ck_hbm.at[lay, pt_ref[g * chunk + k]], pbuf.at[slot, k], sem.at[slot])
                for k in range(chunk)]

    @pl.when(b == 0)
    def _():
        for cp in chunk_copies(0, 0):
            cp.start()

    kmt_s[...] = jnp.zeros(kmt_s.shape, F32)
    lane = lax.broadcasted_iota(jnp.int32, kmt_s.shape, 1)

    def body(c, carry):
        g = b * n_chunks + c
        slot = g % 2

        @pl.when(g + 1 < total_chunks)
        def _():
            for cp in chunk_copies(g + 1, 1 - slot):
                cp.start()

        for cp in chunk_copies(g, slot):
            cp.wait()
        for bb in range(chunk // ppb):
            y = pbuf[slot, bb * ppb]
            for p in range(1, ppb):
                y = y + pbuf[slot, bb * ppb + p]
            col = jnp.sum(y.reshape(hd_all, PAGE_SIZE), axis=-1, keepdims=True) * (1.0 / MOBA_BLOCK)
            blk = c * (chunk // ppb) + bb
            kmt_s[...] = jnp.where(lane == blk, col, kmt_s[...])
        return carry

    lax.fori_loop(0, n_chunks, body, 0)

    gate = jnp.dot(qe_ref[0], kmt_s[...], preferred_element_type=F32, precision=lax.Precision.HIGHEST)
    lane_g = lax.broadcasted_iota(jnp.int32, gate.shape, 1)
    gate = jnp.where(lane_g < nblk, gate, NEG_INF)
    out = jnp.zeros(gate.shape, jnp.int32)
    for s in range(MOBA_TOPK):
        top = jnp.max(gate, axis=-1, keepdims=True)
        first = jnp.min(jnp.where(gate == top, lane_g, LANES), axis=-1, keepdims=True)
        out = jnp.where(lane_g == s, first, out)
        gate = jnp.where(lane_g == first, NEG_INF, gate)
    idx_ref[0] = out


def _sample_gate(page_table, q_exp, ck_t, lay):
    bsz, rows, hd_all = q_exp.shape
    n_pages = page_table.shape[1]
    _, _, n_heads, hd, page = ck_t.shape
    chunk = GATE_PAGE_CHUNK
    assert n_pages % chunk == 0 and chunk % (MOBA_BLOCK // PAGE_SIZE) == 0
    assert n_pages // (MOBA_BLOCK // PAGE_SIZE) <= LANES
    kern = functools.partial(_sample_gate_kernel, lay=lay, n_pages=n_pages)
    return pl.pallas_call(
        kern,
        out_shape=jax.ShapeDtypeStruct((bsz, rows, LANES), jnp.int32),
        grid_spec=pltpu.PrefetchScalarGridSpec(
            num_scalar_prefetch=1,
            grid=(bsz,),
            in_specs=[pl.BlockSpec((1, rows, hd_all), lambda b, pt: (b, 0, 0)),
                      pl.BlockSpec(memory_space=pltpu.HBM)],
            out_specs=pl.BlockSpec((1, rows, LANES), lambda b, pt: (b, 0, 0)),
            scratch_shapes=[pltpu.VMEM((2, chunk, n_heads, hd, page), F32), pltpu.VMEM((hd_all, LANES), F32),
                            pltpu.SemaphoreType.DMA((2,))]),
        compiler_params=_params(("arbitrary",)),
        name="sample_gate",
    )(page_table.reshape(-1), q_exp, ck_t)


def _sample_attn_kernel(pt_ref, idx_ref, q_ref, kn_ref, vn_ref, slope_ref, ck_hbm, cv_hbm, o_ref,
                        kbuf, vbuf, sem, *, lay, n_tok, n_pages, past):
    step = pl.program_id(0)
    n_steps = pl.num_programs(0)
    n_heads = ck_hbm.shape[2]
    hd = ck_hbm.shape[3]
    ppb = MOBA_BLOCK // PAGE_SIZE
    n_seg = n_tok * MOBA_TOPK
    n_keys = n_seg * MOBA_BLOCK
    rows_pad = SUBLANES

    def block_index(st, seg):
        b, h = st // n_heads, st % n_heads
        t, s = seg // MOBA_TOPK, seg % MOBA_TOPK
        return idx_ref[((b * n_tok + t) * n_heads + h) * MOBA_TOPK + s]

    def page_copies(st, slot):
        b, h = st // n_heads, st % n_heads
        out = []
        for seg in range(n_seg):
            blk = block_index(st, seg)
            for pg in range(ppb):
                phys = pt_ref[b * n_pages + blk * ppb + pg]
                dst = pl.ds((seg * ppb + pg) * PAGE_SIZE, PAGE_SIZE)
                out.append(pltpu.make_async_copy(ck_hbm.at[lay, phys, h], kbuf.at[slot, :, dst], sem.at[0, slot]))
                out.append(pltpu.make_async_copy(cv_hbm.at[lay, phys, h], vbuf.at[slot, :, dst], sem.at[1, slot]))
        return out

    slot = step % 2

    @pl.when(step == 0)
    def _():
        for cp in page_copies(step, slot):
            cp.start()

    @pl.when(step + 1 < n_steps)
    def _():
        for cp in page_copies(step + 1, 1 - slot):
            cp.start()

    for cp in page_copies(step, slot):
        cp.wait()

    slope = slope_ref[0]
    scale = hd ** -0.5
    zpad = jnp.zeros((rows_pad - n_tok, hd), F32)
    q = (jnp.concatenate([q_ref[0, 0], zpad], axis=0) * scale).astype(BF16)
    kn = jnp.concatenate([kn_ref[0, 0], zpad], axis=0).astype(BF16)
    vn = jnp.concatenate([vn_ref[0, 0], zpad], axis=0).astype(BF16)

    s_g = jnp.dot(q, kbuf[slot].astype(BF16), preferred_element_type=F32)
    col = lax.broadcasted_iota(jnp.int32, (1, n_keys), 1)
    seg_of_col = col // MOBA_BLOCK
    blk_of_col = jnp.zeros((1, n_keys), jnp.int32)
    for seg in range(n_seg):
        blk_of_col = jnp.where(seg_of_col == seg, block_index(step, seg), blk_of_col)
    kpos = blk_of_col * MOBA_BLOCK + col % MOBA_BLOCK
    row = lax.broadcasted_iota(jnp.int32, (rows_pad, n_keys), 0)
    dist = (past + row - kpos).astype(F32)
    s_g = jnp.where(row == seg_of_col // MOBA_TOPK, s_g - slope[:, :1] * dist, NEG_INF)

    s_n = lax.dot_general(q, kn, (((1,), (1,)), ((), ())), preferred_element_type=F32)
    r_n = lax.broadcasted_iota(jnp.int32, s_n.shape, 0)
    c_n = lax.broadcasted_iota(jnp.int32, s_n.shape, 1)
    s_n = jnp.where((c_n <= r_n) & (c_n < n_tok), s_n - slope[:, :1] * (r_n - c_n).astype(F32), NEG_INF)

    m = jnp.maximum(jnp.max(s_g, axis=-1, keepdims=True), jnp.max(s_n, axis=-1, keepdims=True))
    p_g = jnp.exp(s_g - m)
    p_n = jnp.exp(s_n - m)
    den = jnp.sum(p_g, axis=-1, keepdims=True) + jnp.sum(p_n, axis=-1, keepdims=True)
    o = lax.dot_general(p_g.astype(BF16), vbuf[slot].astype(BF16), (((1,), (1,)), ((), ())),
                        preferred_element_type=F32)
    o = o + jnp.dot(p_n.astype(BF16), vn, preferred_element_type=F32)
    o_ref[0, 0] = (o / den)[:n_tok].astype(o_ref.dtype)


def _sample_attn(page_table, idx_flat, q_h, k_h, v_h, ck_t, cv_t, lay):
    bsz, n_heads, n_tok, hd = q_h.shape
    n_pages = page_table.shape[1]
    past = n_pages * PAGE_SIZE
    n_keys = n_tok * MOBA_TOPK * MOBA_BLOCK
    slopes = jnp.broadcast_to(_alibi_slopes(n_heads)[:, None, None], (n_heads, 1, LANES))
    tok = pl.BlockSpec((1, 1, n_tok, hd), lambda s, pt, ix: (s // n_heads, s % n_heads, 0, 0))
    kern = functools.partial(_sample_attn_kernel, lay=lay, n_tok=n_tok, n_pages=n_pages, past=past)
    return pl.pallas_call(
        kern,
        out_shape=jax.ShapeDtypeStruct((bsz, n_heads, n_tok, hd), BF16),
        grid_spec=pltpu.PrefetchScalarGridSpec(
            num_scalar_prefetch=2,
            grid=(bsz * n_heads,),
            in_specs=[tok, tok, tok,
                      pl.BlockSpec((1, 1, LANES), lambda s, pt, ix: (s % n_heads, 0, 0)),
                      pl.BlockSpec(memory_space=pltpu.HBM), pl.BlockSpec(memory_space=pltpu.HBM)],
            out_specs=tok,
            scratch_shapes=[pltpu.VMEM((2, hd, n_keys), F32), pltpu.VMEM((2, hd, n_keys), F32),
                            pltpu.SemaphoreType.DMA((2, 2))]),
        compiler_params=_params(("arbitrary",)),
        name="sample_attn",
    )(page_table.reshape(-1), idx_flat, q_h, k_h, v_h, slopes, ck_t, cv_t)


def _moba_sample(q, k_new, v_new, ck_t, cv_t, page_table, lay):
    bsz, n_tok, hd_all = q.shape
    n_heads = ck_t.shape[2]
    hd = ck_t.shape[3]
    n_pages = page_table.shape[1]
    ppb = MOBA_BLOCK // PAGE_SIZE
    assert n_pages % ppb == 0, "past pages inside the new tokens' own block are not supported"
    assert n_pages // ppb >= MOBA_TOPK and n_tok <= SUBLANES
    head_of_lane = jnp.arange(hd_all) // hd
    head_mask = (head_of_lane[None, :] == jnp.arange(n_heads)[:, None]).astype(F32)
    q_exp = (q[:, :, None, :] * head_mask[None, None]).reshape(bsz, n_tok * n_heads, hd_all)
    idx = _sample_gate(page_table, q_exp, ck_t, lay)[:, :, :MOBA_TOPK]
    heads_first = lambda a: a.reshape(bsz, n_tok, n_heads, hd).transpose(0, 2, 1, 3)
    o = _sample_attn(page_table, idx.reshape(-1), heads_first(q), heads_first(k_new), heads_first(v_new),
                     ck_t, cv_t, lay)
    return o.transpose(0, 2, 1, 3).reshape(bsz, n_tok, hd_all)


def _prompt_attn(xn, w, n_heads):
    seq = xn.shape[0]
    hd = ATTN_HEAD_DIM
    assert seq % MOBA_BLOCK == 0
    nb = seq // MOBA_BLOCK
    q_t = _matmul_t(w["w_q"].T, xn, BF16)
    k_t = _matmul_t(w["w_k"].T, xn, F32)
    v_t = _matmul_t(w["w_v"].T, xn, F32)
    kmean, k_bf = _kmean(k_t)
    v_t4 = v_t.astype(BF16).reshape(n_heads, hd, nb, MOBA_BLOCK).transpose(0, 2, 1, 3)
    o = _moba_prompt(q_t, k_bf, v_t4, kmean, n_heads).T
    rows_first = lambda a_t: a_t.reshape(n_heads, hd, seq).transpose(2, 0, 1)
    return o, rows_first(k_t), rows_first(v_t)


def _layer(x, attn_fn, conv_state, h0, mem_k, mem_v, w, g_final):
    bsz, seq, d = x.shape
    rows = bsz * seq
    x2 = x.reshape(rows, d)
    xn = _rmsnorm(x2, w["g_norm_mix"], BF16)
    z = _matmul(xn, w["w_z"], F32)
    xbc = _matmul(xn, w["w_xbc"], F32)
    dtp = _matmul(xn, w["w_dt"], F32)
    gate_logits = _matmul(xn, w["w_g"], F32)
    y_ssm, new_conv, new_h = _ssd_branch(
        z.reshape(bsz, seq, -1), xbc.reshape(bsz, seq, -1), dtp.reshape(bsz, seq, -1), conv_state, h0,
        w["conv_w"], w["conv_b"], w["dt_bias"], w["a_log"], w["d_skip"], w["g_ssm_norm"])
    o_attn, k, v = attn_fn(xn.reshape(bsz, seq, d))
    x2, xn = _merge(y_ssm.reshape(rows, -1), o_attn.reshape(rows, -1), gate_logits, x2,
                    w["w_ssm_proj"], w["w_attn_proj"], w["w_out"], w["g_norm_x"])
    xq = _matmul(xn, w["w_xq"], BF16)
    o_x = _xattn(xq.reshape(bsz, seq, d), mem_k, mem_v)
    x2, xn = _matmul_res(o_x.reshape(rows, d), w["w_xo"], x2, w["g_norm_ffn"])
    x2 = _ffn(xn, x2, w["w_up"], w["w_down"], g_final)
    return x2.reshape(bsz, seq, d), k, v, new_conv, new_h


def kernel(x_prompt, x_sample, state_conv, state_ssm, cache_k, cache_v, cache_mem_k, cache_mem_v, page_table, mem_prompt, g_norm_mix, w_in, conv_w, conv_b, dt_bias, a_log, d_skip, g_ssm_norm, w_ssm_proj, w_attn_proj, w_out, g_norm_x, g_mem, w_xq, w_xk, w_xv, w_xo, g_norm_ffn, w_up, w_down, g_final):
    depth = w_in.shape[0]
    bp, seq_p, d = x_prompt.shape
    bd, seq_d, _ = x_sample.shape
    n_heads, hd = cache_k.shape[3], cache_k.shape[4]
    attn_dim = n_heads * hd
    ssm_heads = dt_bias.shape[1]
    d_inner = ssm_heads * SSM_HEAD_DIM
    conv_dim = conv_w.shape[2]
    n_mem = mem_prompt.shape[1]
    xh, xd = cache_mem_k.shape[3], cache_mem_k.shape[4]
    z_off = 0
    xbc_off = z_off + d_inner
    dt_off = xbc_off + conv_dim
    q_off = dt_off + ssm_heads
    k_off = q_off + attn_dim
    v_off = k_off + attn_dim
    g_off = v_off + attn_dim

    ck_t = jnp.transpose(cache_k, (0, 1, 3, 4, 2))
    cv_t = jnp.transpose(cache_v, (0, 1, 3, 4, 2))
    xp, xs = x_prompt, x_sample
    outs = {n: [] for n in ("kp", "vp", "cp", "hp", "mkp", "mvp", "ks", "vs", "cs", "hs")}
    for l in range(depth):
        wl = w_in[l].astype(BF16)
        w = {
            "g_norm_mix": g_norm_mix[l], "g_norm_x": g_norm_x[l], "g_norm_ffn": g_norm_ffn[l],
            "w_z": wl[:, z_off:xbc_off], "w_xbc": wl[:, xbc_off:dt_off],
            "w_dt": jnp.pad(wl[:, dt_off:q_off], ((0, 0), (0, LANES - ssm_heads))),
            "w_q": wl[:, q_off:k_off], "w_k": wl[:, k_off:v_off], "w_v": wl[:, v_off:g_off], "w_g": wl[:, g_off:],
            "conv_w": conv_w[l], "conv_b": conv_b[l], "dt_bias": dt_bias[l], "a_log": a_log[l],
            "d_skip": d_skip[l], "g_ssm_norm": g_ssm_norm[l],
            "w_ssm_proj": w_ssm_proj[l].astype(BF16), "w_attn_proj": w_attn_proj[l].astype(BF16),
            "w_out": w_out[l].astype(BF16), "w_xq": w_xq[l].astype(BF16), "w_xo": w_xo[l].astype(BF16),
            "w_up": w_up[l].astype(BF16), "w_down": w_down[l].astype(BF16),
        }
        last = g_final if l == depth - 1 else None

        mem_n = _rmsnorm(mem_prompt.reshape(bp * n_mem, d), g_mem[l], BF16)
        mk_p = _matmul(mem_n, w_xk[l].astype(BF16), F32).reshape(bp, n_mem, d)
        mv_p = _matmul(mem_n, w_xv[l].astype(BF16), F32).reshape(bp, n_mem, d)
        conv0 = jnp.zeros((bp, CONV_WIDTH - 1, conv_dim), F32)
        h0 = jnp.zeros((bp, ssm_heads, SSM_HEAD_DIM, SSM_STATE), F32)

        def attn_p(xn):
            per_seq = [_prompt_attn(xn[b], w, n_heads) for b in range(bp)]
            return tuple(jnp.stack(parts) for parts in zip(*per_seq))

        xp, kp, vp, cp, hp = _layer(xp, attn_p, conv0, h0, mk_p, mv_p, w, last)

        def attn_s(xn):
            proj = lambda name: _matmul(xn.reshape(bd * seq_d, d), w[name], F32).reshape(bd, seq_d, attn_dim)
            q, k, v = proj("w_q"), proj("w_k"), proj("w_v")
            o = _moba_sample(q, k, v, ck_t, cv_t, page_table, l)
            split = lambda a: a.reshape(bd, seq_d, n_heads, hd)
            return o, split(k), split(v)

        xs, ks, vs, cs, hs = _layer(xs, attn_s, state_conv[l], state_ssm[l],
                                    cache_mem_k[l].reshape(bd, n_mem, d), cache_mem_v[l].reshape(bd, n_mem, d),
                                    w, last)

        outs["kp"].append(kp)
        outs["vp"].append(vp)
        outs["cp"].append(cp)
        outs["hp"].append(hp)
        outs["mkp"].append(mk_p.reshape(bp, n_mem, xh, xd))
        outs["mvp"].append(mv_p.reshape(bp, n_mem, xh, xd))
        outs["ks"].append(ks)
        outs["vs"].append(vs)
        outs["cs"].append(cs)
        outs["hs"].append(hs)

    if depth == 0:
        raise ValueError("at least one layer is required")
    st = lambda n: jnp.stack(outs[n])
    return (xp, xs, st("kp"), st("vp"), st("cp"), st("hp"), st("mkp"), st("mvp"),
            st("ks"), st("vs"), st("cs"), st("hs"))
```
